```python
import jax, jax.numpy as jnp
from jax import lax
import numpy as np

D_MODEL = 1024
BATCH = 8
SEQ = 8192
DEPTH = 4

CHUNK = 64
Q_BLOCK = 128
NORM_EPS = 1e-6

BRANCH_W = 512
N_BRANCH = 3

GLA_HEADS = 4
GLA_DK = 64
GLA_DV = 128
GLA_GATE_RANK = 16
GLA_TAU = 16.0

DSA_HEADS = 8
DSA_Q_RANK = 256
DSA_LAT = 128
DSA_DV = 64
IDX_HEADS = 4
IDX_DIM = 64
DSA_TOPK_MAX = 256

RWKV_HEADS = 8
RWKV_HEAD = 64
RWKV_W_RANK = 64
RWKV_A_RANK = 64
RWKV_G_RANK = 128
RWKV_LN_EPS = 64e-5
RWKV_SIZES = (RWKV_HEADS * RWKV_HEAD, RWKV_HEADS * RWKV_HEAD, RWKV_HEADS * RWKV_HEAD, RWKV_W_RANK, RWKV_A_RANK, RWKV_G_RANK)
RWKV_WIDTH = 3 * RWKV_HEADS * RWKV_HEAD + RWKV_W_RANK + RWKV_A_RANK + RWKV_G_RANK

IN_SIZES = (GLA_HEADS * GLA_DK, GLA_HEADS * GLA_DK, GLA_HEADS * GLA_DV, GLA_GATE_RANK, GLA_HEADS * GLA_DV,
            DSA_Q_RANK, DSA_LAT, IDX_DIM, IDX_HEADS,
            RWKV_WIDTH,
            N_BRANCH * D_MODEL)
IN_WIDTH = (2 * GLA_HEADS * GLA_DK + 2 * GLA_HEADS * GLA_DV + GLA_GATE_RANK
            + DSA_Q_RANK + DSA_LAT + IDX_DIM + IDX_HEADS + RWKV_WIDTH + N_BRANCH * D_MODEL)

PEER_HEADS = 8
PEER_NKEYS = 128
PEER_N = PEER_NKEYS * PEER_NKEYS
PEER_QDIM = 128
PEER_TOPK = 16
PEER_BLOCK = 128

kernel_name = 'hybrid_gla_dsa_rwkv7_peer_adaln'


def _rms(x, g, eps=NORM_EPS):
    xf = x.astype(jnp.float32)
    y = xf * lax.rsqrt(jnp.mean(xf * xf, axis=-1, keepdims=True) + eps)
    return (y * g).astype(x.dtype)


def _split(p, sizes):
    cuts = [int(s) for s in np.cumsum(sizes)[:-1]]
    return jnp.split(p, cuts, axis=-1)


def _shift(p):
    return jnp.pad(p, ((0, 0), (1, 0), (0, 0)))[:, :-1]


def _gla(q, k, v, a_lr, r, w_a2, b_a, g_o):
    f32 = jnp.float32
    B, S, _ = q.shape
    nc = S // CHUNK
    H, dk, dv = GLA_HEADS, GLA_DK, GLA_DV
    q = q.astype(f32).reshape(B, nc, CHUNK, H, dk) * (dk ** -0.5)
    k = k.astype(f32).reshape(B, nc, CHUNK, H, dk)
    v = v.astype(f32).reshape(B, nc, CHUNK, H, dv)
    log_a = jax.nn.log_sigmoid((a_lr @ w_a2 + b_a).astype(f32)) / GLA_TAU
    log_a = log_a.reshape(B, nc, CHUNK, H, dk)
    cum = jnp.cumsum(log_a, axis=2)
    tot = cum[:, :, -1]
    k_dec = k * jnp.exp(tot[:, :, None] - cum)
    upd = jnp.einsum('bnchk,bnchv->nbhkv', k_dec, v)
    chunk_decay = jnp.exp(tot).transpose(1, 0, 2, 3)

    def step(state, inp):
        d, u = inp
        state = d[..., None] * state + u
        return state, state

    _, states = lax.scan(step, jnp.zeros((B, H, dk, dv), f32), (chunk_decay, upd))
    o = jnp.einsum('bnchk,nbhkv->bnchv', q, states).reshape(B, S, H, dv)
    o = _rms(o, g_o).reshape(B, S, H * dv) * jax.nn.silu(r.astype(f32))
    return o


def _dsa(qc, kvc, idx_k, idx_w, g_qc, w_uq, w_iq, g_kv, w_uv):
    f32 = jnp.float32
    B, S, _ = kvc.shape
    topk = min(DSA_TOPK_MAX, S // 4)
    nb = S // Q_BLOCK
    qc = _rms(qc, g_qc)
    q = (qc @ w_uq).reshape(B, nb, Q_BLOCK, DSA_HEADS, DSA_LAT).transpose(1, 0, 2, 3, 4)
    qi = (qc @ w_iq).reshape(B, nb, Q_BLOCK, IDX_HEADS, IDX_DIM).transpose(1, 0, 2, 3, 4)
    wi = idx_w.reshape(B, nb, Q_BLOCK, IDX_HEADS).transpose(1, 0, 2, 3)
    kv = _rms(kvc, g_kv)
    key_chunk = jnp.arange(S) // CHUNK

    def block(args):
        bi, qb, qib, wb = args
        q_chunk = (bi * Q_BLOCK + jnp.arange(Q_BLOCK)) // CHUNK
        admissible = key_chunk[None, :] <= q_chunk[:, None]
        s_idx = jax.nn.relu(jnp.einsum('bthd,bsd->bths', qib, idx_k))
        score = jnp.einsum('bths,bth->bts', s_idx, wb).astype(f32)
        score = jnp.where(admissible[None], score, -jnp.inf)
        top_val, top_idx = lax.top_k(score, topk)
        valid = top_val > -jnp.inf
        kv_sel = jax.vmap(lambda kvb, ib: kvb[ib])(kv, top_idx)
        logits = jnp.einsum('bthd,btkd->bthk', qb, kv_sel).astype(f32) * (DSA_LAT ** -0.5)
        logits = jnp.where(valid[:, :, None, :], logits, -jnp.inf)
        p = jax.nn.softmax(logits, axis=-1).astype(kv.dtype)
        return jnp.einsum('bthk,btkd->bthd', p, kv_sel)

    o = lax.map(block, (jnp.arange(nb), q, qi, wi))
    o = o.transpose(1, 0, 2, 3, 4).reshape(B, S, DSA_HEADS, DSA_LAT)
    o = jnp.einsum('bshd,hde->bshe', o, w_uv).reshape(B, S, DSA_HEADS * DSA_DV)
    return o.astype(f32)


def _rwkv7(p, mu, w0, w_w2, a0, w_a2, w_g2, k_k, k_a, r_k, ln_g, ln_b):
    f32 = jnp.float32
    B, S, _ = p.shape
    H, N = RWKV_HEADS, RWKV_HEAD
    p = p.astype(f32)
    p = p + (_shift(p) - p) * mu
    r, k, v, w_lr, a_lr, g_lr = _split(p, RWKV_SIZES)
    log_w = -jax.nn.softplus(-(w0 + jnp.tanh(w_lr) @ w_w2)) - 0.5
    decay = jnp.exp(-jnp.exp(log_w))
    a = jax.nn.sigmoid(a0 + a_lr @ w_a2)
    g = jax.nn.sigmoid(g_lr) @ w_g2
    kk = (k * k_k).reshape(B, S, H, N)
    kk = kk * lax.rsqrt(jnp.sum(kk * kk, axis=-1, keepdims=True) + 1e-12)
    k = k * (1.0 + (a - 1.0) * k_a)
    r_h, k_h, v_h, a_h, w_h = [t.reshape(B, S, H, N) for t in (r, k, v, a, decay)]

    def step(state, inp):
        r_t, w_t, k_t, v_t, a_t, b_t = inp
        sa = jnp.einsum('bhij,bhj->bhi', state, a_t)
        state = (state * w_t[:, :, None, :] + sa[..., None] * b_t[:, :, None, :]
                 + v_t[..., None] * k_t[:, :, None, :])
        return state, jnp.einsum('bhij,bhj->bhi', state, r_t)

    xs = tuple(jnp.swapaxes(t, 0, 1) for t in (r_h, w_h, k_h, v_h, -kk, kk * a_h))
    _, y = lax.scan(step, jnp.zeros((B, H, N, N), f32), xs)
    y = jnp.swapaxes(y, 0, 1)
    mean = jnp.mean(y, axis=-1, keepdims=True)
    var = jnp.mean(jnp.square(y - mean), axis=-1, keepdims=True)
    yn = ((y - mean) * lax.rsqrt(var + RWKV_LN_EPS)).reshape(B, S, H * N) * ln_g + ln_b
    bonus = (jnp.sum(r_h * k_h * r_k, axis=-1, keepdims=True) * v_h).reshape(B, S, H * N)
    return (yn + bonus) * g


def _peer(h, w_q, sub_keys, u, v):
    f32 = jnp.float32
    B, S, D = h.shape
    nb = (B * S) // PEER_BLOCK
    hb = h.reshape(nb, PEER_BLOCK, D)

    def block(xb):
        q = (xb @ w_q).reshape(PEER_BLOCK, PEER_HEADS, 2, PEER_QDIM // 2)
        s = jnp.einsum('thpd,hpnd->thpn', q, sub_keys).astype(f32)
        s1, i1 = lax.top_k(s[:, :, 0], PEER_TOPK)
        s2, i2 = lax.top_k(s[:, :, 1], PEER_TOPK)
        cand = (s1[..., :, None] + s2[..., None, :]).reshape(PEER_BLOCK, PEER_HEADS, PEER_TOPK * PEER_TOPK)
        cand_idx = (i1[..., :, None] * PEER_NKEYS + i2[..., None, :]).reshape(PEER_BLOCK, PEER_HEADS, PEER_TOPK * PEER_TOPK)
        top_s, pos = lax.top_k(cand, PEER_TOPK)
        e_idx = jnp.take_along_axis(cand_idx, pos, axis=-1)
        gate = jax.nn.softmax(top_s, axis=-1)
        u_sel = u[e_idx]
        v_sel = v[e_idx]
        act = jax.nn.gelu(jnp.einsum('thkd,td->thk', u_sel, xb).astype(f32), approximate=False)
        return jnp.einsum('thk,thkd->td', (gate * act).astype(xb.dtype), v_sel)

    return lax.map(block, hb).reshape(B, S, D)


def setup_inputs(seed: int = 0) -> dict:
    key = jax.random.key(seed)
    ks = iter(jax.random.split(key, 48))
    L, D = DEPTH, D_MODEL
    Wr = RWKV_HEADS * RWKV_HEAD

    def nrm(shape, scale):
        return jax.random.normal(next(ks), shape, jnp.float32) * scale

    def gain(shape):
        return 1.0 + nrm(shape, 0.02)

    return {
        'x': nrm((BATCH, SEQ, D), 1.0),
        'c': nrm((BATCH, D), 1.0),
        'w_ada': nrm((L, D, 6 * D), 0.5 * D ** -0.5),
        'b_ada': nrm((L, 6 * D), 0.01),
        'g_norm_mix': gain((L, D)),
        'g_norm_ffn': gain((L, D)),
        'w_in': nrm((L, D, IN_WIDTH), D ** -0.5),
        'gla_w_a2': nrm((L, GLA_GATE_RANK, GLA_HEADS * GLA_DK), GLA_GATE_RANK ** -0.5),
        'gla_b_a': nrm((L, GLA_HEADS * GLA_DK), 0.01),
        'gla_g_o': gain((L, GLA_DV)),
        'dsa_g_qc': gain((L, DSA_Q_RANK)),
        'dsa_w_uq': nrm((L, DSA_Q_RANK, DSA_HEADS * DSA_LAT), DSA_Q_RANK ** -0.5),
        'dsa_w_iq': nrm((L, DSA_Q_RANK, IDX_HEADS * IDX_DIM), DSA_Q_RANK ** -0.5),
        'dsa_g_kv': gain((L, DSA_LAT)),
        'dsa_w_uv': nrm((L, DSA_HEADS, DSA_LAT, DSA_DV), DSA_LAT ** -0.5),
        'rwkv_mu': jax.random.uniform(next(ks), (L, RWKV_WIDTH), jnp.float32, 0.2, 0.8),
        'rwkv_w0': nrm((L, Wr), 0.5),
        'rwkv_w_w2': nrm((L, RWKV_W_RANK, Wr), RWKV_W_RANK ** -0.5),
        'rwkv_a0': nrm((L, Wr), 0.1),
        'rwkv_w_a2': nrm((L, RWKV_A_RANK, Wr), RWKV_A_RANK ** -0.5),
        'rwkv_w_g2': nrm((L, RWKV_G_RANK, Wr), RWKV_G_RANK ** -0.5),
        'rwkv_k_k': 0.85 + nrm((L, Wr), 0.02),
        'rwkv_k_a': gain((L, Wr)),
        'rwkv_r_k': nrm((L, RWKV_HEADS, RWKV_HEAD), 0.1),
        'rwkv_ln_g': gain((L, Wr)),
        'rwkv_ln_b': nrm((L, Wr), 0.01),
        'w_branch': nrm((L, N_BRANCH, BRANCH_W, D), BRANCH_W ** -0.5),
        'w_out': nrm((L, D, D), D ** -0.5),
        'peer_w_q': nrm((L, D, PEER_HEADS * PEER_QDIM), D ** -0.5),
        'peer_keys': nrm((L, PEER_HEADS, 2, PEER_NKEYS, PEER_QDIM // 2), (PEER_QDIM // 2) ** -0.5),
        'peer_u': nrm((L, PEER_N, D), D ** -0.5),
        'peer_v': nrm((L, PEER_N, D), 0.5),
        'g_final': gain((D,)),
    }


def reference(x, c, w_ada, b_ada, g_norm_mix, g_norm_ffn, w_in, gla_w_a2, gla_b_a, gla_g_o,
              dsa_g_qc, dsa_w_uq, dsa_w_iq, dsa_g_kv, dsa_w_uv,
              rwkv_mu, rwkv_w0, rwkv_w_w2, rwkv_a0, rwkv_w_a2, rwkv_w_g2, rwkv_k_k, rwkv_k_a, rwkv_r_k,
              rwkv_ln_g, rwkv_ln_b, w_branch, w_out, peer_w_q, peer_keys, peer_u, peer_v, g_final):
    B, S, D = x.shape
    cond = jax.nn.silu(c)
    for l in range(DEPTH):
        mod = cond @ w_ada[l] + b_ada[l]
        sh_m, sc_m, gt_m, sh_f, sc_f, gt_f = [m[:, None, :] for m in jnp.split(mod, 6, axis=-1)]
        h = _rms(x, g_norm_mix[l]) * (1.0 + sc_m) + sh_m
        gq, gk, gv, galr, gr, qc, kvc, ik, iw, prw, gates = _split(h @ w_in[l], IN_SIZES)
        y_a = _gla(gq, gk, gv, galr, gr, gla_w_a2[l], gla_b_a[l], gla_g_o[l])
        y_b = _dsa(qc, kvc, ik, iw, dsa_g_qc[l], dsa_w_uq[l], dsa_w_iq[l], dsa_g_kv[l], dsa_w_uv[l])
        y_c = _rwkv7(prw, rwkv_mu[l], rwkv_w0[l], rwkv_w_w2[l], rwkv_a0[l], rwkv_w_a2[l], rwkv_w_g2[l],
                     rwkv_k_k[l], rwkv_k_a[l], rwkv_r_k[l], rwkv_ln_g[l], rwkv_ln_b[l])
        g_a, g_b, g_c = jnp.split(jax.nn.sigmoid(gates), N_BRANCH, axis=-1)
        merged = (g_a * (y_a.astype(x.dtype) @ w_branch[l, 0])
                  + g_b * (y_b.astype(x.dtype) @ w_branch[l, 1])
                  + g_c * (y_c.astype(x.dtype) @ w_branch[l, 2]))
        x = x + gt_m * (merged @ w_out[l])
        h2 = _rms(x, g_norm_ffn[l]) * (1.0 + sc_f) + sh_f
        x = x + gt_f * _peer(h2, peer_w_q[l], peer_keys[l], peer_u[l], peer_v[l])
    return _rms(x, g_final)
```

```python
import functools

import jax
import jax.numpy as jnp
import numpy as np
from jax import lax
from jax.experimental import pallas as pl
from jax.experimental.pallas import tpu as pltpu

CHUNK = 64
Q_BLOCK = 128
NORM_EPS = 1e-6

GLA_HEADS, GLA_DK, GLA_DV, GLA_GATE_RANK, GLA_TAU = 4, 64, 128, 16, 16.0
DSA_HEADS, DSA_Q_RANK, DSA_LAT, DSA_DV = 8, 256, 128, 64
IDX_HEADS, IDX_DIM, DSA_TOPK_MAX = 4, 64, 256
RWKV_HEADS, RWKV_HEAD = 8, 64
RWKV_W_RANK, RWKV_A_RANK, RWKV_G_RANK = 64, 64, 128
RWKV_LN_EPS = 64e-5
RWKV_WR = RWKV_HEADS * RWKV_HEAD
RWKV_SIZES = (RWKV_WR, RWKV_WR, RWKV_WR, RWKV_W_RANK, RWKV_A_RANK, RWKV_G_RANK)
RWKV_WIDTH = sum(RWKV_SIZES)
N_BRANCH = 3
PEER_HEADS, PEER_NKEYS, PEER_QDIM, PEER_TOPK = 8, 128, 128, 16
PEER_N = PEER_NKEYS * PEER_NKEYS

LANES = 128
INT_MIN = int(np.iinfo(np.int32).min)
F32 = jnp.float32
BF16 = jnp.bfloat16
HIGHEST = lax.Precision.HIGHEST


def _in_sizes(d_model):
    return (GLA_HEADS * GLA_DK, GLA_HEADS * GLA_DK, GLA_HEADS * GLA_DV, GLA_GATE_RANK, GLA_HEADS * GLA_DV,
            DSA_Q_RANK, DSA_LAT, IDX_DIM, IDX_HEADS, RWKV_WIDTH, N_BRANCH * d_model)


def _rms(x, g, eps=NORM_EPS):
    xf = x.astype(F32)
    y = xf * lax.rsqrt(jnp.mean(xf * xf, axis=-1, keepdims=True) + eps)
    return (y * g).astype(x.dtype)


def _split(p, sizes):
    cuts = [int(s) for s in np.cumsum(sizes)[:-1]]
    return jnp.split(p, cuts, axis=-1)


def _sortable_key(x):
    b = lax.bitcast_convert_type(x, jnp.int32)
    return b ^ ((b >> 31) & jnp.int32(0x7FFFFFFF))


def _dsa_kernel(qi_ref, wi_ref, q_ref, kT_ref, kvT_ref, kv_ref, wuv_ref, o_ref,
                key_scr, m_scr, l_scr, acc_scr, *, topk):
    QB = Q_BLOCK
    bi = pl.program_id(1)
    nkb = bi + 1
    row = lax.broadcasted_iota(jnp.int32, (QB, QB), 0)
    lane = lax.broadcasted_iota(jnp.int32, (QB, QB), 1)
    q_end = ((bi * QB + row) // CHUNK + 1) * CHUNK
    wb = [jnp.broadcast_to(wi_ref[0, :, h:h + 1], (QB, QB)) for h in range(IDX_HEADS)]

    def score_body(kb, c):
        off = pl.multiple_of(kb * QB, QB)
        ks = kT_ref[0, :, pl.ds(off, QB)]
        sc = jnp.zeros((QB, QB), F32)
        for h in range(IDX_HEADS):
            s = jnp.dot(qi_ref[0, h], ks, precision=HIGHEST, preferred_element_type=F32)
            sc = sc + jnp.maximum(s, 0.0) * wb[h]
        key_scr[kb] = jnp.where(off + lane < q_end, _sortable_key(sc), INT_MIN)
        return c

    lax.fori_loop(0, nkb, score_body, 0)

    def count(pred, thr_b):
        def body(kb, acc):
            return acc + pred(key_scr[kb], thr_b).astype(jnp.int32)
        acc = lax.fori_loop(0, nkb, body, jnp.zeros((QB, QB), jnp.int32))
        return jnp.sum(acc, axis=1, keepdims=True)

    def bit_body(i, thr):
        cand = thr + jnp.left_shift(jnp.int32(1), 31 - i)
        cnt = count(lambda k, t: k >= t, jnp.broadcast_to(cand, (QB, QB)))
        return jnp.where(cnt >= topk, cand, thr)

    thr = lax.fori_loop(0, 32, bit_body, jnp.full((QB, 1), INT_MIN, jnp.int32))
    thr_b = jnp.broadcast_to(thr, (QB, QB))
    n_gt = count(lambda k, t: k > t, thr_b)
    need_b = jnp.broadcast_to((topk - n_gt).astype(F32), (QB, QB))

    tri = (row <= lane).astype(BF16)
    ones = jnp.ones((QB, QB), BF16)

    def mask_body(kb, carry):
        key = key_scr[kb]
        eq = key == thr_b
        eqb = eq.astype(BF16)
        pre = jnp.dot(eqb, tri, preferred_element_type=F32)
        sel = ((key > thr_b) | (eq & (carry + pre <= need_b))) & (key > INT_MIN)
        key_scr[kb] = sel.astype(jnp.int32)
        return carry + jnp.dot(eqb, ones, preferred_element_type=F32)

    lax.fori_loop(0, nkb, mask_body, jnp.zeros((QB, QB), F32))

    m_scr[...] = jnp.full(m_scr.shape, -jnp.inf, F32)

    def max_body(kb, c):
        off = pl.multiple_of(kb * QB, QB)
        kvt = kvT_ref[0, :, pl.ds(off, QB)]
        msk = key_scr[kb] > 0
        for h in range(DSA_HEADS):
            lg = jnp.dot(q_ref[0, :, h * DSA_LAT:(h + 1) * DSA_LAT], kvt, preferred_element_type=F32)
            m_scr[h] = jnp.maximum(m_scr[h], jnp.where(msk, lg, -jnp.inf))
        return c

    lax.fori_loop(0, nkb, max_body, 0)
    for h in range(DSA_HEADS):
        m_scr[h] = jnp.broadcast_to(jnp.max(m_scr[h], axis=1, keepdims=True), (QB, QB))
    l_scr[...] = jnp.zeros(l_scr.shape, F32)
    acc_scr[...] = jnp.zeros(acc_scr.shape, F32)
    scale = DSA_LAT ** -0.5

    def acc_body(kb, c):
        off = pl.multiple_of(kb * QB, QB)
        kvt = kvT_ref[0, :, pl.ds(off, QB)]
        kvb = kv_ref[0, pl.ds(off, QB), :]
        msk = key_scr[kb] > 0
        for h in range(DSA_HEADS):
            lg = jnp.dot(q_ref[0, :, h * DSA_LAT:(h + 1) * DSA_LAT], kvt, preferred_element_type=F32)
            p = jnp.where(msk, jnp.exp((lg - m_scr[h]) * scale), 0.0)
            l_scr[h] += p
            acc_scr[h] += jnp.dot(p.astype(BF16), kvb, preferred_element_type=F32)
        return c

    lax.fori_loop(0, nkb, acc_body, 0)
    for h in range(DSA_HEADS):
        o = acc_scr[h] / jnp.sum(l_scr[h], axis=1, keepdims=True)
        o_ref[0, :, h * DSA_DV:(h + 1) * DSA_DV] = jnp.dot(
            o.astype(BF16), wuv_ref[h], preferred_element_type=F32)


def _dsa(qc, kvc, idx_k, idx_w, g_qc, w_uq, w_iq, g_kv, w_uv):
    B, S, _ = kvc.shape
    topk = min(DSA_TOPK_MAX, S // 4)
    nb = S // Q_BLOCK
    qcn = _rms(qc, g_qc)
    q = jnp.dot(qcn, w_uq).astype(BF16)
    qi = jnp.dot(qcn, w_iq, precision=HIGHEST)
    qi_t = qi.reshape(B, S, IDX_HEADS, IDX_DIM).transpose(0, 2, 1, 3)
    kv = _rms(kvc, g_kv).astype(BF16)
    kvT = kv.transpose(0, 2, 1)
    kT = idx_k.transpose(0, 2, 1)
    return pl.pallas_call(
        functools.partial(_dsa_kernel, topk=topk),
        grid=(B, nb),
        in_specs=[
            pl.BlockSpec((1, IDX_HEADS, Q_BLOCK, IDX_DIM), lambda b, i: (b, 0, i, 0)),
            pl.BlockSpec((1, Q_BLOCK, IDX_HEADS), lambda b, i: (b, i, 0)),
            pl.BlockSpec((1, Q_BLOCK, DSA_HEADS * DSA_LAT), lambda b, i: (b, i, 0)),
            pl.BlockSpec((1, IDX_DIM, S), lambda b, i: (b, 0, 0)),
            pl.BlockSpec((1, DSA_LAT, S), lambda b, i: (b, 0, 0)),
            pl.BlockSpec((1, S, DSA_LAT), lambda b, i: (b, 0, 0)),
            pl.BlockSpec((DSA_HEADS, DSA_LAT, DSA_DV), lambda b, i: (0, 0, 0)),
        ],
        out_specs=pl.BlockSpec((1, Q_BLOCK, DSA_HEADS * DSA_DV), lambda b, i: (b, i, 0)),
        out_shape=jax.ShapeDtypeStruct((B, S, DSA_HEADS * DSA_DV), F32),
        scratch_shapes=[
            pltpu.VMEM((nb, Q_BLOCK, Q_BLOCK), jnp.int32),
            pltpu.VMEM((DSA_HEADS, Q_BLOCK, Q_BLOCK), F32),
            pltpu.VMEM((DSA_HEADS, Q_BLOCK, Q_BLOCK), F32),
            pltpu.VMEM((DSA_HEADS, Q_BLOCK, DSA_LAT), F32),
        ],
        compiler_params=pltpu.CompilerParams(
            dimension_semantics=("arbitrary", "arbitrary"),
            vmem_limit_bytes=48 * 1024 * 1024),
        name="dsa_fused",
    )(qi_t, idx_w, q, kT, kvT, kv, w_uv.astype(BF16))


def _extract_topk(s, k):
    R = s.shape[0]
    rid = lax.broadcasted_iota(jnp.int32, s.shape, 0)
    rows = []
    cur = s
    for _ in range(k):
        m = jnp.max(cur, axis=0, keepdims=True)
        rows.append(m)
        first = jnp.min(jnp.where(cur == m, rid, R), axis=0, keepdims=True)
        cur = jnp.where(rid == first, -jnp.inf, cur)
    return jnp.concatenate(rows, axis=0)


def _peer_route_kernel(hT_ref, wqT_ref, keys_ref, s1_ref, s2_ref, e1_ref, e2_ref, tau_ref):
    half = PEER_QDIM // 2
    qT = jnp.dot(wqT_ref[...], hT_ref[...], precision=HIGHEST, preferred_element_type=F32)
    for h in range(PEER_HEADS):
        s = []
        for p in range(2):
            r0 = (h * 2 + p) * half
            s.append(jnp.dot(keys_ref[h * 2 + p], qT[r0:r0 + half], precision=HIGHEST,
                             preferred_element_type=F32))
        a = _extract_topk(s[0], PEER_TOPK)
        b = _extract_topk(s[1], PEER_TOPK)
        c = jnp.concatenate([a[i:i + 1] + b for i in range(PEER_TOPK)], axis=0)
        tau = _extract_topk(c, PEER_TOPK)[PEER_TOPK - 1:PEER_TOPK]
        m1, m2 = a[0:1], b[0:1]
        z = jnp.sum(jnp.where(c >= tau, jnp.exp(c - (m1 + m2)), 0.0), axis=0, keepdims=True)
        s1_ref[h] = s[0]
        s2_ref[h] = s[1]
        e1_ref[h] = jnp.exp(s[0] - m1) / z
        e2_ref[h] = jnp.exp(s[1] - m2)
        tau_ref[h] = tau


def _peer_dense_kernel(hT_ref, u_ref, vT_ref, s1_ref, s2_ref, e1_ref, e2_ref, tau_ref, o_ref, *, eb_rows):
    eb = pl.program_id(1)
    TT = hT_ref.shape[1]

    @pl.when(eb == 0)
    def _():
        o_ref[...] = jnp.zeros(o_ref.shape, F32)

    a = jnp.dot(u_ref[...], hT_ref[...], preferred_element_type=F32)
    g = 0.5 * a * (1.0 + lax.erf(a * (2.0 ** -0.5)))
    ws = []
    for ii in range(eb_rows):
        i = eb * eb_rows + ii
        w = jnp.zeros((PEER_NKEYS, TT), F32)
        for h in range(PEER_HEADS):
            s1row = s1_ref[h, pl.ds(i, 1), :]
            e1row = e1_ref[h, pl.ds(i, 1), :]
            w = w + jnp.where(s2_ref[h] + s1row >= tau_ref[h], e2_ref[h] * e1row, 0.0)
        ws.append(w)
    wT = jnp.concatenate(ws, axis=0)
    o_ref[...] += jnp.dot(vT_ref[...], (wT * g).astype(BF16), preferred_element_type=F32)


def _peer(h2, w_q, sub_keys, u_bf, vT_bf):
    B, S, D = h2.shape
    N = B * S
    TT = 256 if N % 256 == 0 else LANES
    EB = 512
    hT = h2.reshape(N, D).T
    keys = sub_keys.reshape(PEER_HEADS * 2, PEER_NKEYS, PEER_QDIM // 2)
    stat = jax.ShapeDtypeStruct((PEER_HEADS, PEER_NKEYS, N), F32)
    stat_spec = pl.BlockSpec((PEER_HEADS, PEER_NKEYS, TT), lambda t: (0, 0, t))
    s1, s2, e1, e2, tau = pl.pallas_call(
        _peer_route_kernel,
        grid=(N // TT,),
        in_specs=[
            pl.BlockSpec((D, TT), lambda t: (0, t)),
            pl.BlockSpec((PEER_HEADS * PEER_QDIM, D), lambda t: (0, 0)),
            pl.BlockSpec((PEER_HEADS * 2, PEER_NKEYS, PEER_QDIM // 2), lambda t: (0, 0, 0)),
        ],
        out_specs=[stat_spec, stat_spec, stat_spec, stat_spec,
                   pl.BlockSpec((PEER_HEADS, 1, TT), lambda t: (0, 0, t))],
        out_shape=[stat, stat, stat, stat, jax.ShapeDtypeStruct((PEER_HEADS, 1, N), F32)],
        compiler_params=pltpu.CompilerParams(
            dimension_semantics=("arbitrary",), vmem_limit_bytes=48 * 1024 * 1024),
        name="peer_route",
    )(hT, w_q.T, keys)

    TD = 512 if N % 512 == 0 else TT
    stat_spec2 = pl.BlockSpec((PEER_HEADS, PEER_NKEYS, TD), lambda t, e: (0, 0, t))
    oT = pl.pallas_call(
        functools.partial(_peer_dense_kernel, eb_rows=EB // PEER_NKEYS),
        grid=(N // TD, PEER_N // EB),
        in_specs=[
            pl.BlockSpec((D, TD), lambda t, e: (0, t)),
            pl.BlockSpec((EB, D), lambda t, e: (e, 0)),
            pl.BlockSpec((D, EB), lambda t, e: (0, e)),
            stat_spec2, stat_spec2, stat_spec2, stat_spec2,
            pl.BlockSpec((PEER_HEADS, 1, TD), lambda t, e: (0, 0, t)),
        ],
        out_specs=pl.BlockSpec((D, TD), lambda t, e: (0, t)),
        out_shape=jax.ShapeDtypeStruct((D, N), F32),
        compiler_params=pltpu.CompilerParams(
            dimension_semantics=("arbitrary", "arbitrary"), vmem_limit_bytes=56 * 1024 * 1024),
        name="peer_dense",
    )(hT.astype(BF16), u_bf, vT_bf, s1, s2, e1, e2, tau)
    return oT.T.reshape(B, S, D)


def _gla(q, k, v, a_lr, r, w_a2, b_a, g_o):
    B, S, _ = q.shape
    nc = S // CHUNK
    H, dk, dv = GLA_HEADS, GLA_DK, GLA_DV
    q = q.reshape(B, nc, CHUNK, H, dk) * (dk ** -0.5)
    k = k.reshape(B, nc, CHUNK, H, dk)
    v = v.reshape(B, nc, CHUNK, H, dv)
    log_a = jax.nn.log_sigmoid(a_lr @ w_a2 + b_a) / GLA_TAU
    log_a = log_a.reshape(B, nc, CHUNK, H, dk)
    cum = jnp.cumsum(log_a, axis=2)
    tot = cum[:, :, -1]
    k_dec = k * jnp.exp(tot[:, :, None] - cum)
    upd = jnp.einsum('bnchk,bnchv->nbhkv', k_dec, v)
    chunk_decay = jnp.exp(tot).transpose(1, 0, 2, 3)

    def step(state, inp):
        d, u = inp
        state = d[..., None] * state + u
        return state, state

    _, states = lax.scan(step, jnp.zeros((B, H, dk, dv), F32), (chunk_decay, upd))
    o = jnp.einsum('bnchk,nbhkv->bnchv', q, states).reshape(B, S, H, dv)
    return _rms(o, g_o).reshape(B, S, H * dv) * jax.nn.silu(r)


def _rwkv7(p, mu, w0, w_w2, a0, w_a2, w_g2, k_k, k_a, r_k, ln_g, ln_b):
    B, S, _ = p.shape
    H, N = RWKV_HEADS, RWKV_HEAD
    prev = jnp.pad(p, ((0, 0), (1, 0), (0, 0)))[:, :-1]
    p = p + (prev - p) * mu
    r, k, v, w_lr, a_lr, g_lr = _split(p, RWKV_SIZES)
    log_w = -jax.nn.softplus(-(w0 + jnp.tanh(w_lr) @ w_w2)) - 0.5
    decay = jnp.exp(-jnp.exp(log_w))
    a = jax.nn.sigmoid(a0 + a_lr @ w_a2)
    g = jax.nn.sigmoid(g_lr) @ w_g2
    kk = (k * k_k).reshape(B, S, H, N)
    kk = kk * lax.rsqrt(jnp.sum(kk * kk, axis=-1, keepdims=True) + 1e-12)
    k = k * (1.0 + (a - 1.0) * k_a)
    r_h, k_h, v_h, a_h, w_h = [t.reshape(B, S, H, N) for t in (r, k, v, a, decay)]

    def step(state, inp):
        r_t, w_t, k_t, v_t, a_t, b_t = inp
        sa = jnp.einsum('bhij,bhj->bhi', state, a_t)
        state = (state * w_t[:, :, None, :] + sa[..., None] * b_t[:, :, None, :]
                 + v_t[..., None] * k_t[:, :, None, :])
        return state, jnp.einsum('bhij,bhj->bhi', state, r_t)

    xs = tuple(jnp.swapaxes(t, 0, 1) for t in (r_h, w_h, k_h, v_h, -kk, kk * a_h))
    _, y = lax.scan(step, jnp.zeros((B, H, N, N), F32), xs)
    y = jnp.swapaxes(y, 0, 1)
    mean = jnp.mean(y, axis=-1, keepdims=True)
    var = jnp.mean(jnp.square(y - mean), axis=-1, keepdims=True)
    yn = ((y - mean) * lax.rsqrt(var + RWKV_LN_EPS)).reshape(B, S, H * N) * ln_g + ln_b
    bonus = (jnp.sum(r_h * k_h * r_k, axis=-1, keepdims=True) * v_h).reshape(B, S, H * N)
    return (yn + bonus) * g


def kernel(x, c, w_ada, b_ada, g_norm_mix, g_norm_ffn, w_in, gla_w_a2, gla_b_a, gla_g_o, dsa_g_qc, dsa_w_uq, dsa_w_iq, dsa_g_kv, dsa_w_uv, rwkv_mu, rwkv_w0, rwkv_w_w2, rwkv_a0, rwkv_w_a2, rwkv_w_g2, rwkv_k_k, rwkv_k_a, rwkv_r_k, rwkv_ln_g, rwkv_ln_b, w_branch, w_out, peer_w_q, peer_keys, peer_u, peer_v, g_final):
    B, S, D = x.shape
    depth = w_in.shape[0]
    in_sizes = _in_sizes(D)
    cond = jax.nn.silu(c)
    u_bf = peer_u.astype(BF16)
    vT_bf = jnp.swapaxes(peer_v, 1, 2).astype(BF16)
    for l in range(depth):
        mod = cond @ w_ada[l] + b_ada[l]
        sh_m, sc_m, gt_m, sh_f, sc_f, gt_f = [m[:, None, :] for m in jnp.split(mod, 6, axis=-1)]
        h = _rms(x, g_norm_mix[l]) * (1.0 + sc_m) + sh_m
        gq, gk, gv, galr, gr, qc, kvc, ik, iw, prw, gates = _split(h @ w_in[l], in_sizes)
        y_a = _gla(gq, gk, gv, galr, gr, gla_w_a2[l], gla_b_a[l], gla_g_o[l])
        y_b = _dsa(qc, kvc, ik, iw, dsa_g_qc[l], dsa_w_uq[l], dsa_w_iq[l], dsa_g_kv[l], dsa_w_uv[l])
        y_c = _rwkv7(prw, rwkv_mu[l], rwkv_w0[l], rwkv_w_w2[l], rwkv_a0[l], rwkv_w_a2[l], rwkv_w_g2[l],
                     rwkv_k_k[l], rwkv_k_a[l], rwkv_r_k[l], rwkv_ln_g[l], rwkv_ln_b[l])
        g_a, g_b, g_c = jnp.split(jax.nn.sigmoid(gates), N_BRANCH, axis=-1)
        merged = (g_a * (y_a @ w_branch[l, 0]) + g_b * (y_b @ w_branch[l, 1]) + g_c * (y_c @ w_branch[l, 2]))
        x = x + gt_m * (merged @ w_out[l])
        h2 = _rms(x, g_norm_ffn[l]) * (1.0 + sc_f) + sh_f
        x = x + gt_f * _peer(h2, peer_w_q[l], peer_keys[l], u_bf[l], vT_bf[l])
    return _rms(x, g_final)
```

```python
import functools

import jax
import jax.numpy as jnp
import numpy as np
from jax import lax
from jax.experimental import pallas as pl
from jax.experimental.pallas import tpu as pltpu

CHUNK = 64
Q_BLOCK = 128
NORM_EPS = 1e-6

GLA_HEADS, GLA_DK, GLA_DV, GLA_GATE_RANK, GLA_TAU = 4, 64, 128, 16, 16.0
DSA_HEADS, DSA_Q_RANK, DSA_LAT, DSA_DV = 8, 256, 128, 64
IDX_HEADS, IDX_DIM, DSA_TOPK_MAX = 4, 64, 256
RWKV_HEADS, RWKV_HEAD = 8, 64
RWKV_W_RANK, RWKV_A_RANK, RWKV_G_RANK = 64, 64, 128
RWKV_LN_EPS = 64e-5
RWKV_CHUNK = 64
RWKV_WR = RWKV_HEADS * RWKV_HEAD
RWKV_SIZES = (RWKV_WR, RWKV_WR, RWKV_WR, RWKV_W_RANK, RWKV_A_RANK, RWKV_G_RANK)
RWKV_WIDTH = sum(RWKV_SIZES)
N_BRANCH = 3
PEER_HEADS, PEER_NKEYS, PEER_QDIM, PEER_TOPK = 8, 128, 128, 16
PEER_N = PEER_NKEYS * PEER_NKEYS

LANES = 128
DSA_KEY_BLOCK = 256
INT_MIN = int(np.iinfo(np.int32).min)
F32 = jnp.float32
BF16 = jnp.bfloat16
HIGHEST = lax.Precision.HIGHEST


def _in_sizes(d_model):
    return (GLA_HEADS * GLA_DK, GLA_HEADS * GLA_DK, GLA_HEADS * GLA_DV, GLA_GATE_RANK, GLA_HEADS * GLA_DV,
            DSA_Q_RANK, DSA_LAT, IDX_DIM, IDX_HEADS, RWKV_WIDTH, N_BRANCH * d_model)


def _rms(x, g, eps=NORM_EPS):
    xf = x.astype(F32)
    y = xf * lax.rsqrt(jnp.mean(xf * xf, axis=-1, keepdims=True) + eps)
    return (y * g).astype(x.dtype)


def _split(p, sizes):
    cuts = [int(s) for s in np.cumsum(sizes)[:-1]]
    return jnp.split(p, cuts, axis=-1)


def _split_bf16(x):
    hi = x.astype(BF16)
    return hi, (x - hi.astype(F32)).astype(BF16)


def _sortable_key(x):
    b = lax.bitcast_convert_type(x, jnp.int32)
    return b ^ ((b >> 31) & jnp.int32(0x7FFFFFFF))


def _dsa_kernel(q3_ref, wi_ref, q_ref, k3_ref, kvT_ref, kv_ref, wuv_ref, o_ref,
                key_scr, m_scr, l_scr, acc_scr, *, topk):
    QB, KB = Q_BLOCK, DSA_KEY_BLOCK
    bi = pl.program_id(1)
    nkb = (bi * QB + QB + KB - 1) // KB
    row = lax.broadcasted_iota(jnp.int32, (QB, KB), 0)
    lane = lax.broadcasted_iota(jnp.int32, (QB, KB), 1)
    q_end = ((bi * QB + row) // CHUNK + 1) * CHUNK
    wb = [jnp.broadcast_to(wi_ref[0, :, h:h + 1], (QB, KB)) for h in range(IDX_HEADS)]
    q3 = q3_ref[0].reshape(IDX_HEADS * QB, 3 * IDX_DIM)

    def key_off(kb):
        return pl.multiple_of(kb * KB, KB)

    def fold(x):
        return sum(x[:, g * LANES:(g + 1) * LANES] for g in range(1, KB // LANES)) + x[:, :LANES]

    def score_body(kb, c):
        s = jnp.dot(q3, k3_ref[0, :, pl.ds(key_off(kb), KB)], preferred_element_type=F32)
        sc = jnp.zeros((QB, KB), F32)
        for h in range(IDX_HEADS):
            sc = sc + jnp.maximum(s[h * QB:(h + 1) * QB], 0.0) * wb[h]
        key_scr[kb] = jnp.where(key_off(kb) + lane < q_end, _sortable_key(sc), INT_MIN)
        return c

    lax.fori_loop(0, nkb, score_body, 0)

    def count(pred, thr):
        def body(kb, acc):
            return acc + fold(pred(key_scr[kb], thr).astype(jnp.int32))
        acc = lax.fori_loop(0, nkb, body, jnp.zeros((QB, LANES), jnp.int32))
        return jnp.sum(acc, axis=1, keepdims=True)

    n_adm = q_end[:, 0:1]
    open_row = n_adm > topk

    def any_row(x):
        return jnp.max(jnp.where(x, 1, 0))

    def bit_cond(st):
        i, _, _, busy = st
        return jnp.logical_and(i < 32, busy > 0)

    def bit_body(st):
        i, thr, cnt, _ = st
        cand = thr + jnp.left_shift(jnp.int32(1), 31 - i)
        c = count(lambda k, t: k >= t, cand)
        take = c >= topk
        thr = jnp.where(take, cand, thr)
        cnt = jnp.where(take, c, cnt)
        return i + 1, thr, cnt, any_row(open_row & (cnt != topk))

    _, thr, cnt, _ = lax.while_loop(
        bit_cond, bit_body,
        (jnp.int32(0), jnp.full((QB, 1), INT_MIN, jnp.int32), n_adm, any_row(open_row)))

    def tie_path():
        n_gt = count(lambda k, t: k > t, thr)
        need = (topk - n_gt).astype(F32)
        r2 = lax.broadcasted_iota(jnp.int32, (KB, KB), 0)
        c2 = lax.broadcasted_iota(jnp.int32, (KB, KB), 1)
        tri = (r2 <= c2).astype(BF16)
        ones = jnp.ones((KB, KB), BF16)

        def mask_body(kb, carry):
            key = key_scr[kb]
            eq = key == thr
            eqb = eq.astype(BF16)
            pre = jnp.dot(eqb, tri, preferred_element_type=F32)
            sel = ((key > thr) | (eq & (carry + pre <= need))) & (key > INT_MIN)
            key_scr[kb] = jnp.where(sel, jnp.int32(1), jnp.int32(INT_MIN))
            return carry + jnp.dot(eqb, ones, preferred_element_type=F32)

        lax.fori_loop(0, nkb, mask_body, jnp.zeros((QB, KB), F32))
        return jnp.zeros((QB, 1), jnp.int32)

    thr_sel = lax.cond(any_row(open_row & (cnt > topk)) > 0, tie_path,
                       lambda: jnp.maximum(thr, INT_MIN + 1))

    scale = DSA_LAT ** -0.5

    def logits(h, kb):
        return jnp.dot(q_ref[0, :, h * DSA_LAT:(h + 1) * DSA_LAT], kvT_ref[0, :, pl.ds(key_off(kb), KB)],
                       preferred_element_type=F32)

    m_scr[...] = jnp.full(m_scr.shape, -jnp.inf, F32)

    def max_body(kb, c):
        sel = key_scr[kb] >= thr_sel
        for h in range(DSA_HEADS):
            x = jnp.where(sel, logits(h, kb), -jnp.inf)
            xm = x[:, :LANES]
            for g in range(1, KB // LANES):
                xm = jnp.maximum(xm, x[:, g * LANES:(g + 1) * LANES])
            m_scr[h] = jnp.maximum(m_scr[h], xm)
        return c

    lax.fori_loop(0, nkb, max_body, 0)
    for h in range(DSA_HEADS):
        m_scr[h] = jnp.broadcast_to(jnp.max(m_scr[h], axis=1, keepdims=True), (QB, LANES))
    l_scr[...] = jnp.zeros(l_scr.shape, F32)
    acc_scr[...] = jnp.zeros(acc_scr.shape, F32)

    def acc_body(kb, c):
        sel = key_scr[kb] >= thr_sel
        kvb = kv_ref[0, pl.ds(key_off(kb), KB), :]
        for h in range(DSA_HEADS):
            mb = jnp.concatenate([m_scr[h]] * (KB // LANES), axis=1)
            p = jnp.where(sel, jnp.exp((logits(h, kb) - mb) * scale), 0.0)
            l_scr[h] += fold(p)
            acc_scr[h] += jnp.dot(p.astype(BF16), kvb, preferred_element_type=F32)
        return c

    lax.fori_loop(0, nkb, acc_body, 0)
    for h in range(DSA_HEADS):
        o = acc_scr[h] / jnp.sum(l_scr[h], axis=1, keepdims=True)
        o_ref[0, :, h * DSA_DV:(h + 1) * DSA_DV] = jnp.dot(
            o.astype(BF16), wuv_ref[h], preferred_element_type=F32)


def _dsa(qc, kvc, idx_k, idx_w, g_qc, w_uq, w_iq, g_kv, w_uv):
    B, S, _ = kvc.shape
    topk = min(DSA_TOPK_MAX, S // 4)
    nb = S // Q_BLOCK
    qcn = _rms(qc, g_qc)
    q = jnp.dot(qcn, w_uq).astype(BF16)
    qi = jnp.dot(qcn, w_iq, precision=HIGHEST)
    qi_t = qi.reshape(B, S, IDX_HEADS, IDX_DIM).transpose(0, 2, 1, 3)
    q_hi, q_lo = _split_bf16(qi_t)
    q3 = jnp.concatenate([q_hi, q_lo, q_hi], axis=-1)
    k_hi, k_lo = _split_bf16(idx_k.transpose(0, 2, 1))
    k3 = jnp.concatenate([k_hi, k_hi, k_lo], axis=1)
    kv = _rms(kvc, g_kv).astype(BF16)
    kvT = kv.transpose(0, 2, 1)
    return pl.pallas_call(
        functools.partial(_dsa_kernel, topk=topk),
        grid=(B, nb),
        in_specs=[
            pl.BlockSpec((1, IDX_HEADS, Q_BLOCK, 3 * IDX_DIM), lambda b, i: (b, 0, i, 0)),
            pl.BlockSpec((1, Q_BLOCK, IDX_HEADS), lambda b, i: (b, i, 0)),
            pl.BlockSpec((1, Q_BLOCK, DSA_HEADS * DSA_LAT), lambda b, i: (b, i, 0)),
            pl.BlockSpec((1, 3 * IDX_DIM, S), lambda b, i: (b, 0, 0)),
            pl.BlockSpec((1, DSA_LAT, S), lambda b, i: (b, 0, 0)),
            pl.BlockSpec((1, S, DSA_LAT), lambda b, i: (b, 0, 0)),
            pl.BlockSpec((DSA_HEADS, DSA_LAT, DSA_DV), lambda b, i: (0, 0, 0)),
        ],
        out_specs=pl.BlockSpec((1, Q_BLOCK, DSA_HEADS * DSA_DV), lambda b, i: (b, i, 0)),
        out_shape=jax.ShapeDtypeStruct((B, S, DSA_HEADS * DSA_DV), F32),
        scratch_shapes=[
            pltpu.VMEM((S // DSA_KEY_BLOCK, Q_BLOCK, DSA_KEY_BLOCK), jnp.int32),
            pltpu.VMEM((DSA_HEADS, Q_BLOCK, LANES), F32),
            pltpu.VMEM((DSA_HEADS, Q_BLOCK, LANES), F32),
            pltpu.VMEM((DSA_HEADS, Q_BLOCK, DSA_LAT), F32),
        ],
        compiler_params=pltpu.CompilerParams(
            dimension_semantics=("arbitrary", "arbitrary"),
            vmem_limit_bytes=48 * 1024 * 1024),
        name="dsa_fused",
    )(q3, idx_w, q, k3, kvT, kv, w_uv.astype(BF16))


def _extract_topk(s, k):
    R = s.shape[0]
    rid = lax.broadcasted_iota(jnp.int32, s.shape, 0)
    rows = []
    cur = s
    for _ in range(k):
        m = jnp.max(cur, axis=0, keepdims=True)
        rows.append(m)
        first = jnp.min(jnp.where(cur == m, rid, R), axis=0, keepdims=True)
        cur = jnp.where(rid == first, -jnp.inf, cur)
    return jnp.concatenate(rows, axis=0)


def _peer_route_kernel(hT_ref, wqT_ref, keys_ref, s1_ref, s2_ref, e1_ref, e2_ref, tau_ref):
    half = PEER_QDIM // 2
    qT = jnp.dot(wqT_ref[...], hT_ref[...], precision=HIGHEST, preferred_element_type=F32)
    for h in range(PEER_HEADS):
        s = []
        for p in range(2):
            r0 = (h * 2 + p) * half
            s.append(jnp.dot(keys_ref[h * 2 + p], qT[r0:r0 + half], precision=HIGHEST,
                             preferred_element_type=F32))
        a = _extract_topk(s[0], PEER_TOPK)
        b = _extract_topk(s[1], PEER_TOPK)
        half_k = PEER_TOPK // 2
        c = jnp.concatenate(
            [a + b[0:1]] + [a[:half_k] + b[j:j + 1] for j in range(1, half_k)] + [a[0:1] + b[half_k:]], axis=0)
        tau = _extract_topk(c, PEER_TOPK)[PEER_TOPK - 1:PEER_TOPK]
        m1, m2 = a[0:1], b[0:1]
        z = jnp.sum(jnp.where(c >= tau, jnp.exp(c - (m1 + m2)), 0.0), axis=0, keepdims=True)
        s1_ref[h] = s[0]
        s2_ref[h] = s[1]
        e1_ref[h] = jnp.exp(s[0] - m1) / z
        e2_ref[h] = jnp.exp(s[1] - m2)
        tau_ref[h] = tau


def _peer_dense_kernel(hT_ref, u_ref, vT_ref, s1_ref, s2_ref, e1_ref, e2_ref, tau_ref, o_ref, *, eb_rows):
    eb = pl.program_id(1)
    TT = hT_ref.shape[1]

    @pl.when(eb == 0)
    def _():
        o_ref[...] = jnp.zeros(o_ref.shape, F32)

    a = jnp.dot(u_ref[...], hT_ref[...], preferred_element_type=F32)
    g = 0.5 * a * (1.0 + lax.erf(a * (2.0 ** -0.5)))
    ws = []
    for ii in range(eb_rows):
        i = eb * eb_rows + ii
        w = jnp.zeros((PEER_NKEYS, TT), F32)
        for h in range(PEER_HEADS):
            s1row = s1_ref[h, pl.ds(i, 1), :]
            e1row = e1_ref[h, pl.ds(i, 1), :]
            w = w + jnp.where(s2_ref[h] + s1row >= tau_ref[h], e2_ref[h] * e1row, 0.0)
        ws.append(w)
    wT = jnp.concatenate(ws, axis=0)
    o_ref[...] += jnp.dot(vT_ref[...], (wT * g).astype(BF16), preferred_element_type=F32)


def _peer(h2, w_q, sub_keys, u_bf, vT_bf):
    B, S, D = h2.shape
    N = B * S
    TT = 256 if N % 256 == 0 else LANES
    EB = 512
    hT = h2.reshape(N, D).T
    keys = sub_keys.reshape(PEER_HEADS * 2, PEER_NKEYS, PEER_QDIM // 2)
    stat = jax.ShapeDtypeStruct((PEER_HEADS, PEER_NKEYS, N), F32)
    stat_spec = pl.BlockSpec((PEER_HEADS, PEER_NKEYS, TT), lambda t: (0, 0, t))
    s1, s2, e1, e2, tau = pl.pallas_call(
        _peer_route_kernel,
        grid=(N // TT,),
        in_specs=[
            pl.BlockSpec((D, TT), lambda t: (0, t)),
            pl.BlockSpec((PEER_HEADS * PEER_QDIM, D), lambda t: (0, 0)),
            pl.BlockSpec((PEER_HEADS * 2, PEER_NKEYS, PEER_QDIM // 2), lambda t: (0, 0, 0)),
        ],
        out_specs=[stat_spec, stat_spec, stat_spec, stat_spec,
                   pl.BlockSpec((PEER_HEADS, 1, TT), lambda t: (0, 0, t))],
        out_shape=[stat, stat, stat, stat, jax.ShapeDtypeStruct((PEER_HEADS, 1, N), F32)],
        compiler_params=pltpu.CompilerParams(
            dimension_semantics=("arbitrary",), vmem_limit_bytes=48 * 1024 * 1024),
        name="peer_route",
    )(hT, w_q.T, keys)

    TD = 512 if N % 512 == 0 else TT
    stat_spec2 = pl.BlockSpec((PEER_HEADS, PEER_NKEYS, TD), lambda t, e: (0, 0, t))
    oT = pl.pallas_call(
        functools.partial(_peer_dense_kernel, eb_rows=EB // PEER_NKEYS),
        grid=(N // TD, PEER_N // EB),
        in_specs=[
            pl.BlockSpec((D, TD), lambda t, e: (0, t)),
            pl.BlockSpec((EB, D), lambda t, e: (e, 0)),
            pl.BlockSpec((D, EB), lambda t, e: (0, e)),
            stat_spec2, stat_spec2, stat_spec2, stat_spec2,
            pl.BlockSpec((PEER_HEADS, 1, TD), lambda t, e: (0, 0, t)),
        ],
        out_specs=pl.BlockSpec((D, TD), lambda t, e: (0, t)),
        out_shape=jax.ShapeDtypeStruct((D, N), F32),
        compiler_params=pltpu.CompilerParams(
            dimension_semantics=("arbitrary", "arbitrary"), vmem_limit_bytes=56 * 1024 * 1024),
        name="peer_dense",
    )(hT.astype(BF16), u_bf, vT_bf, s1, s2, e1, e2, tau)
    return oT.T.reshape(B, S, D)


def _gla(q, k, v, a_lr, r, w_a2, b_a, g_o):
    B, S, _ = q.shape
    nc = S // CHUNK
    H, dk, dv = GLA_HEADS, GLA_DK, GLA_DV
    q = q.reshape(B, nc, CHUNK, H, dk) * (dk ** -0.5)
    k = k.reshape(B, nc, CHUNK, H, dk)
    v = v.reshape(B, nc, CHUNK, H, dv)
    log_a = jax.nn.log_sigmoid(a_lr @ w_a2 + b_a) / GLA_TAU
    log_a = log_a.reshape(B, nc, CHUNK, H, dk)
    cum = jnp.cumsum(log_a, axis=2)
    tot = cum[:, :, -1]
    k_dec = k * jnp.exp(tot[:, :, None] - cum)
    upd = jnp.einsum('bnchk,bnchv->nbhkv', k_dec, v)
    chunk_decay = jnp.exp(tot).transpose(1, 0, 2, 3)

    def step(state, inp):
        d, u = inp
        state = d[..., None] * state + u
        return state, state

    _, states = lax.scan(step, jnp.zeros((B, H, dk, dv), F32), (chunk_decay, upd))
    o = jnp.einsum('bnchk,nbhkv->bnchv', q, states).reshape(B, S, H, dv)
    return _rms(o, g_o).reshape(B, S, H * dv) * jax.nn.silu(r)


def _dot_f32(x, y, dims=((1,), (0,))):
    return lax.dot_general(x, y, (dims, ((), ())), precision=HIGHEST, preferred_element_type=F32)


def _rwkv_chunk_kernel(r_ref, lw_ref, k_ref, v_ref, a_ref, b_ref, y_ref, st_scr):
    H, C = r_ref.shape[1], r_ref.shape[2]

    @pl.when(pl.program_id(1) == 0)
    def _():
        st_scr[...] = jnp.zeros(st_scr.shape, F32)

    row = lax.broadcasted_iota(jnp.int32, (C, C), 0)
    col = lax.broadcasted_iota(jnp.int32, (C, C), 1)
    lower = col <= row
    strict = col < row
    ltri = lower.astype(F32)
    eye = (row == col).astype(F32)
    for h in range(H):
        lw, r, k, v, a, b = (ref[0, h] for ref in (lw_ref, r_ref, k_ref, v_ref, a_ref, b_ref))
        cum = _dot_f32(ltri, lw)
        g_in, g_inv = jnp.exp(cum), jnp.exp(-cum)
        g_end = jnp.exp(cum[C - 1:C] - cum)
        ar = jnp.concatenate([a * jnp.exp(cum - lw), r * g_in], axis=0)
        bk = jnp.concatenate([b * g_inv, k * g_inv], axis=0)
        p = _dot_f32(ar, bk, ((1,), (1,)))
        ab = jnp.where(strict, p[:C, :C], 0.0)
        ak = jnp.where(strict, p[:C, C:], 0.0)
        rb = jnp.where(lower, p[C:, :C], 0.0)
        rk = jnp.where(lower, p[C:, C:], 0.0)
        s0 = st_scr[h]
        ars = _dot_f32(ar, s0, ((1,), (1,)))
        inv, pw = eye + ab, ab
        for _ in range(int(np.log2(C)) - 1):
            pw = _dot_f32(pw, pw)
            inv = inv + _dot_f32(inv, pw)
        u = _dot_f32(inv, ars[:C] + _dot_f32(ak, v))
        y_ref[0, h] = ars[C:] + _dot_f32(rb, u) + _dot_f32(rk, v)
        uv = jnp.concatenate([u, v], axis=0)
        bkd = jnp.concatenate([b * g_end, k * g_end], axis=0)
        st_scr[h] = s0 * g_in[C - 1:C] + _dot_f32(uv, bkd, ((0,), (0,)))


def _rwkv7(p, mu, w0, w_w2, a0, w_a2, w_g2, k_k, k_a, r_k, ln_g, ln_b):
    B, S, _ = p.shape
    H, N = RWKV_HEADS, RWKV_HEAD
    prev = jnp.pad(p, ((0, 0), (1, 0), (0, 0)))[:, :-1]
    p = p + (prev - p) * mu
    r, k, v, w_lr, a_lr, g_lr = _split(p, RWKV_SIZES)
    log_w = -jax.nn.softplus(-(w0 + jnp.tanh(w_lr) @ w_w2)) - 0.5
    log_decay = -jnp.exp(log_w)
    a = jax.nn.sigmoid(a0 + a_lr @ w_a2)
    g = jax.nn.sigmoid(g_lr) @ w_g2
    kk = (k * k_k).reshape(B, S, H, N)
    kk = kk * lax.rsqrt(jnp.sum(kk * kk, axis=-1, keepdims=True) + 1e-12)
    k = k * (1.0 + (a - 1.0) * k_a)
    r_h, k_h, v_h, a_h, lw_h = [t.reshape(B, S, H, N) for t in (r, k, v, a, log_decay)]
    heads_first = lambda t: jnp.swapaxes(t, 1, 2)
    C = RWKV_CHUNK
    blk = pl.BlockSpec((1, H, C, N), lambda b, c: (b, 0, c, 0))
    y = pl.pallas_call(
        _rwkv_chunk_kernel,
        grid=(B, S // C),
        in_specs=[blk] * 6,
        out_specs=blk,
        out_shape=jax.ShapeDtypeStruct((B, H, S, N), F32),
        scratch_shapes=[pltpu.VMEM((H, N, N), F32)],
        compiler_params=pltpu.CompilerParams(dimension_semantics=("arbitrary", "arbitrary")),
        name="rwkv7_chunk",
    )(*[heads_first(t) for t in (r_h, lw_h, k_h, v_h, -kk, kk * a_h)])
    y = jnp.swapaxes(y, 1, 2)
    mean = jnp.mean(y, axis=-1, keepdims=True)
    var = jnp.mean(jnp.square(y - mean), axis=-1, keepdims=True)
    yn = ((y - mean) * lax.rsqrt(var + RWKV_LN_EPS)).reshape(B, S, H * N) * ln_g + ln_b
    bonus = (jnp.sum(r_h * k_h * r_k, axis=-1, keepdims=True) * v_h).reshape(B, S, H * N)
    return (yn + bonus) * g


def kernel(x, c, w_ada, b_ada, g_norm_mix, g_norm_ffn, w_in, gla_w_a2, gla_b_a, gla_g_o, dsa_g_qc, dsa_w_uq, dsa_w_iq, dsa_g_kv, dsa_w_uv, rwkv_mu, rwkv_w0, rwkv_w_w2, rwkv_a0, rwkv_w_a2, rwkv_w_g2, rwkv_k_k, rwkv_k_a, rwkv_r_k, rwkv_ln_g, rwkv_ln_b, w_branch, w_out, peer_w_q, peer_keys, peer_u, peer_v, g_final):
    B, S, D = x.shape
    depth = w_in.shape[0]
    in_sizes = _in_sizes(D)
    cond = jax.nn.silu(c)
    u_bf = peer_u.astype(BF16)
    vT_bf = jnp.swapaxes(peer_v, 1, 2).astype(BF16)
    for l in range(depth):
        mod = cond @ w_ada[l] + b_ada[l]
        sh_m, sc_m, gt_m, sh_f, sc_f, gt_f = [m[:, None, :] for m in jnp.split(mod, 6, axis=-1)]
        h = _rms(x, g_norm_mix[l]) * (1.0 + sc_m) + sh_m
        gq, gk, gv, galr, gr, qc, kvc, ik, iw, prw, gates = _split(h @ w_in[l], in_sizes)
        y_a = _gla(gq, gk, gv, galr, gr, gla_w_a2[l], gla_b_a[l], gla_g_o[l])
        y_b = _dsa(qc, kvc, ik, iw, dsa_g_qc[l], dsa_w_uq[l], dsa_w_iq[l], dsa_g_kv[l], dsa_w_uv[l])
        y_c = _rwkv7(prw, rwkv_mu[l], rwkv_w0[l], rwkv_w_w2[l], rwkv_a0[l], rwkv_w_a2[l], rwkv_w_g2[l],
                     rwkv_k_k[l], rwkv_k_a[l], rwkv_r_k[l], rwkv_ln_g[l], rwkv_ln_b[l])
        g_a, g_b, g_c = jnp.split(jax.nn.sigmoid(gates), N_BRANCH, axis=-1)
        merged = (g_a * (y_a @ w_branch[l, 0]) + g_b * (y_b @ w_branch[l, 1]) + g_c * (y_c @ w_branch[l, 2]))
        x = x + gt_m * (merged @ w_out[l])
        h2 = _rms(x, g_norm_ffn[l]) * (1.0 + sc_f) + sh_f
        x = x + gt_f * _peer(h2, peer_w_q[l], peer_keys[l], u_bf[l], vT_bf[l])
    return _rms(x, g_final)
```

```python
import functools

import jax
import jax.numpy as jnp
import numpy as np
from jax import lax
from jax.experimental import pallas as pl
from jax.experimental.pallas import tpu as pltpu

CHUNK = 64
Q_BLOCK = 128
NORM_EPS = 1e-6

GLA_HEADS, GLA_DK, GLA_DV, GLA_GATE_RANK, GLA_TAU = 4, 64, 128, 16, 16.0
DSA_HEADS, DSA_Q_RANK, DSA_LAT, DSA_DV = 8, 256, 128, 64
IDX_HEADS, IDX_DIM, DSA_TOPK_MAX = 4, 64, 256
RWKV_HEADS, RWKV_HEAD = 8, 64
RWKV_W_RANK, RWKV_A_RANK, RWKV_G_RANK = 64, 64, 128
RWKV_LN_EPS = 64e-5
RWKV_CHUNK = 64
RWKV_WR = RWKV_HEADS * RWKV_HEAD
RWKV_SIZES = (RWKV_WR, RWKV_WR, RWKV_WR, RWKV_W_RANK, RWKV_A_RANK, RWKV_G_RANK)
RWKV_WIDTH = sum(RWKV_SIZES)
N_BRANCH = 3
PEER_HEADS, PEER_NKEYS, PEER_QDIM, PEER_TOPK = 8, 128, 128, 16
PEER_N = PEER_NKEYS * PEER_NKEYS

LANES = 128
DSA_KEY_BLOCK = 256
INT_MIN = int(np.iinfo(np.int32).min)
F32 = jnp.float32
BF16 = jnp.bfloat16
HIGHEST = lax.Precision.HIGHEST


def _in_sizes(d_model):
    return (GLA_HEADS * GLA_DK, GLA_HEADS * GLA_DK, GLA_HEADS * GLA_DV, GLA_GATE_RANK, GLA_HEADS * GLA_DV,
            DSA_Q_RANK, DSA_LAT, IDX_DIM, IDX_HEADS, RWKV_WIDTH, N_BRANCH * d_model)


def _rms(x, g, eps=NORM_EPS):
    xf = x.astype(F32)
    y = xf * lax.rsqrt(jnp.mean(xf * xf, axis=-1, keepdims=True) + eps)
    return (y * g).astype(x.dtype)


def _split(p, sizes):
    cuts = [int(s) for s in np.cumsum(sizes)[:-1]]
    return jnp.split(p, cuts, axis=-1)


def _split_bf16(x):
    hi = x.astype(BF16)
    return hi, (x - hi.astype(F32)).astype(BF16)


def _sortable_key(x):
    b = lax.bitcast_convert_type(x, jnp.int32)
    return b ^ ((b >> 31) & jnp.int32(0x7FFFFFFF))


def _dsa_kernel(q3_ref, wi_ref, q_ref, k3_ref, kvT_ref, kv_ref, wuv_ref, o_ref,
                key_scr, m_scr, l_scr, acc_scr, *, topk):
    QB, KB = Q_BLOCK, DSA_KEY_BLOCK
    bi = pl.program_id(1)
    nkb = (bi * QB + QB + KB - 1) // KB
    row = lax.broadcasted_iota(jnp.int32, (QB, KB), 0)
    lane = lax.broadcasted_iota(jnp.int32, (QB, KB), 1)
    q_end = ((bi * QB + row) // CHUNK + 1) * CHUNK
    wb = [jnp.broadcast_to(wi_ref[0, :, h:h + 1], (QB, KB)) for h in range(IDX_HEADS)]
    q3 = q3_ref[0].reshape(IDX_HEADS * QB, 3 * IDX_DIM)

    def key_off(kb):
        return pl.multiple_of(kb * KB, KB)

    def fold(x):
        return sum(x[:, g * LANES:(g + 1) * LANES] for g in range(1, KB // LANES)) + x[:, :LANES]

    def score_body(kb, c):
        s = jnp.dot(q3, k3_ref[0, :, pl.ds(key_off(kb), KB)], preferred_element_type=F32)
        sc = jnp.zeros((QB, KB), F32)
        for h in range(IDX_HEADS):
            sc = sc + jnp.maximum(s[h * QB:(h + 1) * QB], 0.0) * wb[h]
        key_scr[kb] = jnp.where(key_off(kb) + lane < q_end, _sortable_key(sc), INT_MIN)
        return c

    lax.fori_loop(0, nkb, score_body, 0)

    def count(pred, thr):
        def body(kb, acc):
            return acc + fold(pred(key_scr[kb], thr).astype(jnp.int32))
        acc = lax.fori_loop(0, nkb, body, jnp.zeros((QB, LANES), jnp.int32))
        return jnp.sum(acc, axis=1, keepdims=True)

    n_adm = q_end[:, 0:1]
    open_row = n_adm > topk

    def any_row(x):
        return jnp.max(jnp.where(x, 1, 0))

    def bit_cond(st):
        i, _, _, busy = st
        return jnp.logical_and(i < 32, busy > 0)

    def bit_body(st):
        i, thr, cnt, _ = st
        cand = thr + jnp.left_shift(jnp.int32(1), 31 - i)
        c = count(lambda k, t: k >= t, cand)
        take = c >= topk
        thr = jnp.where(take, cand, thr)
        cnt = jnp.where(take, c, cnt)
        return i + 1, thr, cnt, any_row(open_row & (cnt != topk))

    _, thr, cnt, _ = lax.while_loop(
        bit_cond, bit_body,
        (jnp.int32(0), jnp.full((QB, 1), INT_MIN, jnp.int32), n_adm, any_row(open_row)))

    def tie_path():
        n_gt = count(lambda k, t: k > t, thr)
        need = (topk - n_gt).astype(F32)
        r2 = lax.broadcasted_iota(jnp.int32, (KB, KB), 0)
        c2 = lax.broadcasted_iota(jnp.int32, (KB, KB), 1)
        tri = (r2 <= c2).astype(BF16)
        ones = jnp.ones((KB, KB), BF16)

        def mask_body(kb, carry):
            key = key_scr[kb]
            eq = key == thr
            eqb = eq.astype(BF16)
            pre = jnp.dot(eqb, tri, preferred_element_type=F32)
            sel = ((key > thr) | (eq & (carry + pre <= need))) & (key > INT_MIN)
            key_scr[kb] = jnp.where(sel, jnp.int32(1), jnp.int32(INT_MIN))
            return carry + jnp.dot(eqb, ones, preferred_element_type=F32)

        lax.fori_loop(0, nkb, mask_body, jnp.zeros((QB, KB), F32))
        return jnp.zeros((QB, 1), jnp.int32)

    thr_sel = lax.cond(any_row(open_row & (cnt > topk)) > 0, tie_path,
                       lambda: jnp.maximum(thr, INT_MIN + 1))

    scale = DSA_LAT ** -0.5

    def logits(kb):
        return jnp.dot(q_ref[0, 0], kvT_ref[0, :, pl.ds(key_off(kb), KB)], preferred_element_type=F32)

    m_scr[...] = jnp.full(m_scr.shape, -jnp.inf, F32)

    def max_body(kb, c):
        sel = key_scr[kb] >= thr_sel
        lg = logits(kb)
        for h in range(DSA_HEADS):
            x = jnp.where(sel, lg[h * QB:(h + 1) * QB], -jnp.inf)
            xm = x[:, :LANES]
            for g in range(1, KB // LANES):
                xm = jnp.maximum(xm, x[:, g * LANES:(g + 1) * LANES])
            m_scr[h] = jnp.maximum(m_scr[h], xm)
        return c

    lax.fori_loop(0, nkb, max_body, 0)
    for h in range(DSA_HEADS):
        m_scr[h] = jnp.broadcast_to(jnp.max(m_scr[h], axis=1, keepdims=True), (QB, LANES))
    l_scr[...] = jnp.zeros(l_scr.shape, F32)
    acc_scr[...] = jnp.zeros(acc_scr.shape, F32)

    def acc_body(kb, c):
        sel = key_scr[kb] >= thr_sel
        kvb = kv_ref[0, pl.ds(key_off(kb), KB), :]
        lg = logits(kb)
        ps = []
        for h in range(DSA_HEADS):
            mb = jnp.concatenate([m_scr[h]] * (KB // LANES), axis=1)
            p = jnp.where(sel, jnp.exp((lg[h * QB:(h + 1) * QB] - mb) * scale), 0.0)
            l_scr[h] += fold(p)
            ps.append(p.astype(BF16))
        acc_scr[...] += jnp.dot(jnp.concatenate(ps, axis=0), kvb, preferred_element_type=F32)
        return c

    lax.fori_loop(0, nkb, acc_body, 0)
    for h in range(DSA_HEADS):
        o = acc_scr[h * QB:(h + 1) * QB] / jnp.sum(l_scr[h], axis=1, keepdims=True)
        o_ref[0, :, h * DSA_DV:(h + 1) * DSA_DV] = jnp.dot(
            o.astype(BF16), wuv_ref[h], preferred_element_type=F32)


def _dsa(qc, kvc, idx_k, idx_w, g_qc, w_uq, w_iq, g_kv, w_uv):
    B, S, _ = kvc.shape
    topk = min(DSA_TOPK_MAX, S // 4)
    nb = S // Q_BLOCK
    qcn = _rms(qc, g_qc)
    q = jnp.dot(qcn, w_uq).astype(BF16)
    q_rows = q.reshape(B, nb, Q_BLOCK, DSA_HEADS, DSA_LAT).transpose(0, 1, 3, 2, 4).reshape(
        B, nb, DSA_HEADS * Q_BLOCK, DSA_LAT)
    qi = jnp.dot(qcn, w_iq, precision=HIGHEST)
    qi_t = qi.reshape(B, S, IDX_HEADS, IDX_DIM).transpose(0, 2, 1, 3)
    q_hi, q_lo = _split_bf16(qi_t)
    q3 = jnp.concatenate([q_hi, q_lo, q_hi], axis=-1)
    k_hi, k_lo = _split_bf16(idx_k.transpose(0, 2, 1))
    k3 = jnp.concatenate([k_hi, k_hi, k_lo], axis=1)
    kv = _rms(kvc, g_kv).astype(BF16)
    kvT = kv.transpose(0, 2, 1)
    return pl.pallas_call(
        functools.partial(_dsa_kernel, topk=topk),
        grid=(B, nb),
        in_specs=[
            pl.BlockSpec((1, IDX_HEADS, Q_BLOCK, 3 * IDX_DIM), lambda b, i: (b, 0, i, 0)),
            pl.BlockSpec((1, Q_BLOCK, IDX_HEADS), lambda b, i: (b, i, 0)),
            pl.BlockSpec((1, 1, DSA_HEADS * Q_BLOCK, DSA_LAT), lambda b, i: (b, i, 0, 0)),
            pl.BlockSpec((1, 3 * IDX_DIM, S), lambda b, i: (b, 0, 0)),
            pl.BlockSpec((1, DSA_LAT, S), lambda b, i: (b, 0, 0)),
            pl.BlockSpec((1, S, DSA_LAT), lambda b, i: (b, 0, 0)),
            pl.BlockSpec((DSA_HEADS, DSA_LAT, DSA_DV), lambda b, i: (0, 0, 0)),
        ],
        out_specs=pl.BlockSpec((1, Q_BLOCK, DSA_HEADS * DSA_DV), lambda b, i: (b, i, 0)),
        out_shape=jax.ShapeDtypeStruct((B, S, DSA_HEADS * DSA_DV), F32),
        scratch_shapes=[
            pltpu.VMEM((S // DSA_KEY_BLOCK, Q_BLOCK, DSA_KEY_BLOCK), jnp.int32),
            pltpu.VMEM((DSA_HEADS, Q_BLOCK, LANES), F32),
            pltpu.VMEM((DSA_HEADS, Q_BLOCK, LANES), F32),
            pltpu.VMEM((DSA_HEADS * Q_BLOCK, DSA_LAT), F32),
        ],
        compiler_params=pltpu.CompilerParams(
            dimension_semantics=("arbitrary", "arbitrary"),
            vmem_limit_bytes=48 * 1024 * 1024),
        name="dsa_fused",
    )(q3, idx_w, q_rows, k3, kvT, kv, w_uv.astype(BF16))


def _extract_topk(s, k):
    R = s.shape[0]
    rid = lax.broadcasted_iota(jnp.int32, s.shape, 0)
    rows = []
    cur = s
    for _ in range(k):
        m = jnp.max(cur, axis=0, keepdims=True)
        rows.append(m)
        first = jnp.min(jnp.where(cur == m, rid, R), axis=0, keepdims=True)
        cur = jnp.where(rid == first, -jnp.inf, cur)
    return jnp.concatenate(rows, axis=0)


def _peer_route_kernel(hT_ref, wqT_ref, keys_ref, s1_ref, s2_ref, e1_ref, e2_ref, tau_ref):
    half = PEER_QDIM // 2
    qT = jnp.dot(wqT_ref[...], hT_ref[...], precision=HIGHEST, preferred_element_type=F32)
    for h in range(PEER_HEADS):
        s = []
        for p in range(2):
            r0 = (h * 2 + p) * half
            s.append(jnp.dot(keys_ref[h * 2 + p], qT[r0:r0 + half], precision=HIGHEST,
                             preferred_element_type=F32))
        a = _extract_topk(s[0], PEER_TOPK)
        b = _extract_topk(s[1], PEER_TOPK)
        half_k = PEER_TOPK // 2
        c = jnp.concatenate(
            [a + b[0:1]] + [a[:half_k] + b[j:j + 1] for j in range(1, half_k)] + [a[0:1] + b[half_k:]], axis=0)
        tau = _extract_topk(c, PEER_TOPK)[PEER_TOPK - 1:PEER_TOPK]
        m1, m2 = a[0:1], b[0:1]
        z = jnp.sum(jnp.where(c >= tau, jnp.exp(c - (m1 + m2)), 0.0), axis=0, keepdims=True)
        s1_ref[h] = s[0]
        s2_ref[h] = s[1]
        e1_ref[h] = jnp.exp(s[0] - m1) / z
        e2_ref[h] = jnp.exp(s[1] - m2)
        tau_ref[h] = tau


def _peer_dense_kernel(hT_ref, u_ref, vT_ref, s1_ref, s2_ref, e1_ref, e2_ref, tau_ref, o_ref,
                       a_scr, gw_scr, *, eb_rows):
    eb = pl.program_id(1)
    TT = hT_ref.shape[1]

    @pl.when(eb == 0)
    def _():
        o_ref[...] = jnp.zeros(o_ref.shape, F32)

    a_scr[...] = jnp.dot(u_ref[...], hT_ref[...], preferred_element_type=F32)
    JH = PEER_NKEYS // 2
    for ts in range(TT // LANES):
        tsl = slice(ts * LANES, (ts + 1) * LANES)
        for jh in range(2):
            jsl = slice(jh * JH, (jh + 1) * JH)
            accs = [jnp.zeros((JH, LANES), F32) for _ in range(eb_rows)]
            for h in range(PEER_HEADS):
                s2t, e2t, taut = s2_ref[h, jsl, tsl], e2_ref[h, jsl, tsl], tau_ref[h, :, tsl]
                for ii in range(eb_rows):
                    i = eb * eb_rows + ii
                    s1row = s1_ref[h, pl.ds(i, 1), :][:, tsl]
                    e1row = e1_ref[h, pl.ds(i, 1), :][:, tsl]
                    accs[ii] = accs[ii] + jnp.where(s2t + s1row >= taut, e2t * e1row, 0.0)
            for ii in range(eb_rows):
                rsl = slice(ii * PEER_NKEYS + jh * JH, ii * PEER_NKEYS + (jh + 1) * JH)
                a = a_scr[rsl, tsl]
                g = 0.5 * a * (1.0 + lax.erf(a * (2.0 ** -0.5)))
                gw_scr[rsl, tsl] = (accs[ii] * g).astype(BF16)
    o_ref[...] += jnp.dot(vT_ref[...], gw_scr[...], preferred_element_type=F32)


def _peer(h2, w_q, sub_keys, u_bf, vT_bf):
    B, S, D = h2.shape
    N = B * S
    TT = 256 if N % 256 == 0 else LANES
    EB = 512
    hT = h2.reshape(N, D).T
    keys = sub_keys.reshape(PEER_HEADS * 2, PEER_NKEYS, PEER_QDIM // 2)
    stat = jax.ShapeDtypeStruct((PEER_HEADS, PEER_NKEYS, N), F32)
    stat_spec = pl.BlockSpec((PEER_HEADS, PEER_NKEYS, TT), lambda t: (0, 0, t))
    s1, s2, e1, e2, tau = pl.pallas_call(
        _peer_route_kernel,
        grid=(N // TT,),
        in_specs=[
            pl.BlockSpec((D, TT), lambda t: (0, t)),
            pl.BlockSpec((PEER_HEADS * PEER_QDIM, D), lambda t: (0, 0)),
            pl.BlockSpec((PEER_HEADS * 2, PEER_NKEYS, PEER_QDIM // 2), lambda t: (0, 0, 0)),
        ],
        out_specs=[stat_spec, stat_spec, stat_spec, stat_spec,
                   pl.BlockSpec((PEER_HEADS, 1, TT), lambda t: (0, 0, t))],
        out_shape=[stat, stat, stat, stat, jax.ShapeDtypeStruct((PEER_HEADS, 1, N), F32)],
        compiler_params=pltpu.CompilerParams(
            dimension_semantics=("arbitrary",), vmem_limit_bytes=48 * 1024 * 1024),
        name="peer_route",
    )(hT, w_q.T, keys)

    TD = 512 if N % 512 == 0 else TT
    stat_spec2 = pl.BlockSpec((PEER_HEADS, PEER_NKEYS, TD), lambda t, e: (0, 0, t))
    oT = pl.pallas_call(
        functools.partial(_peer_dense_kernel, eb_rows=EB // PEER_NKEYS),
        grid=(N // TD, PEER_N // EB),
        in_specs=[
            pl.BlockSpec((D, TD), lambda t, e: (0, t)),
            pl.BlockSpec((EB, D), lambda t, e: (e, 0)),
            pl.BlockSpec((D, EB), lambda t, e: (0, e)),
            stat_spec2, stat_spec2, stat_spec2, stat_spec2,
            pl.BlockSpec((PEER_HEADS, 1, TD), lambda t, e: (0, 0, t)),
        ],
        out_specs=pl.BlockSpec((D, TD), lambda t, e: (0, t)),
        out_shape=jax.ShapeDtypeStruct((D, N), F32),
        scratch_shapes=[pltpu.VMEM((EB, TD), F32), pltpu.VMEM((EB, TD), BF16)],
        compiler_params=pltpu.CompilerParams(
            dimension_semantics=("arbitrary", "arbitrary"), vmem_limit_bytes=56 * 1024 * 1024),
        name="peer_dense",
    )(hT.astype(BF16), u_bf, vT_bf, s1, s2, e1, e2, tau)
    return oT.T.reshape(B, S, D)


def _gla(q, k, v, a_lr, r, w_a2, b_a, g_o):
    B, S, _ = q.shape
    nc = S // CHUNK
    H, dk, dv = GLA_HEADS, GLA_DK, GLA_DV
    q = q.reshape(B, nc, CHUNK, H, dk) * (dk ** -0.5)
    k = k.reshape(B, nc, CHUNK, H, dk)
    v = v.reshape(B, nc, CHUNK, H, dv)
    log_a = jax.nn.log_sigmoid(a_lr @ w_a2 + b_a) / GLA_TAU
    log_a = log_a.reshape(B, nc, CHUNK, H, dk)
    cum = jnp.cumsum(log_a, axis=2)
    tot = cum[:, :, -1]
    k_dec = k * jnp.exp(tot[:, :, None] - cum)
    upd = jnp.einsum('bnchk,bnchv->nbhkv', k_dec, v)
    chunk_decay = jnp.exp(tot).transpose(1, 0, 2, 3)

    def step(state, inp):
        d, u = inp
        state = d[..., None] * state + u
        return state, state

    _, states = lax.scan(step, jnp.zeros((B, H, dk, dv), F32), (chunk_decay, upd))
    o = jnp.einsum('bnchk,nbhkv->bnchv', q, states).reshape(B, S, H, dv)
    return _rms(o, g_o).reshape(B, S, H * dv) * jax.nn.silu(r)


def _dot_f32(x, y, dims=((1,), (0,))):
    xh, xl = x if isinstance(x, tuple) else _split_bf16(x)
    yh, yl = y if isinstance(y, tuple) else _split_bf16(y)
    mm = lambda p, q: lax.dot_general(p, q, (dims, ((), ())), preferred_element_type=F32)
    return mm(xh, yh) + (mm(xl, yh) + mm(xh, yl))


def _cumsum_rows(ltri_bf, x):
    h1, r1 = _split_bf16(x)
    h2 = r1.astype(F32)
    r2 = x - h1.astype(F32) - h2
    mm = lambda q: jnp.dot(ltri_bf, q, preferred_element_type=F32)
    return mm(h1) + (mm(r1) + mm(r2.astype(BF16)))


def _rwkv_chunk_kernel(r_ref, lw_ref, k_ref, v_ref, a_ref, b_ref, y_ref, st_scr):
    H, C = r_ref.shape[1], r_ref.shape[2]

    @pl.when(pl.program_id(1) == 0)
    def _():
        st_scr[...] = jnp.zeros(st_scr.shape, F32)

    row = lax.broadcasted_iota(jnp.int32, (C, C), 0)
    col = lax.broadcasted_iota(jnp.int32, (C, C), 1)
    lower = col <= row
    strict = col < row
    ltri = lower.astype(BF16)
    eye = (row == col).astype(F32)
    heads = range(H)
    each = lambda f, *xs: [f(*t) for t in zip(*xs)]
    nt = ((1,), (1,))
    lw, r, k, v, a, b = ([ref[0, h] for h in heads] for ref in (lw_ref, r_ref, k_ref, v_ref, a_ref, b_ref))
    s0 = [st_scr[h] for h in heads]
    cum = each(lambda x: _cumsum_rows(ltri, x), lw)
    g_in = each(jnp.exp, cum)
    g_inv = each(lambda c: jnp.exp(-c), cum)
    ar = each(lambda a_, r_, c, l, g: _split_bf16(jnp.concatenate([a_ * jnp.exp(c - l), r_ * g], axis=0)),
              a, r, cum, lw, g_in)
    bk = each(lambda b_, k_, g: jnp.concatenate([b_ * g, k_ * g], axis=0), b, k, g_inv)
    v_s = each(_split_bf16, v)
    p = each(lambda x, y: _dot_f32(x, y, nt), ar, bk)
    ars = each(lambda x, s: _dot_f32(x, s, nt), ar, s0)
    ab = each(lambda q: jnp.where(strict, q[:C, :C], 0.0), p)
    ak = each(lambda q: jnp.where(strict, q[:C, C:], 0.0), p)
    rb = each(lambda q: jnp.where(lower, q[C:, :C], 0.0), p)
    rk = each(lambda q: jnp.where(lower, q[C:, C:], 0.0), p)
    rhs = each(lambda x, m, vs: x[:C] + _dot_f32(m, vs), ars, ak, v_s)
    inv, pw = each(lambda m: eye + m, ab), ab
    for _ in range(int(np.log2(C)) - 1):
        pw = each(lambda m: (lambda s: _dot_f32(s, s))(_split_bf16(m)), pw)
        inv = each(lambda i_, m: i_ + _dot_f32(i_, m), inv, pw)
    u = each(_dot_f32, inv, rhs)
    y = each(lambda x, m1, u_, m2, vs: x[C:] + _dot_f32(m1, u_) + _dot_f32(m2, vs), ars, rb, u, rk, v_s)
    for h in heads:
        y_ref[0, h] = y[h]
    g_end = each(lambda c: jnp.exp(c[C - 1:C] - c), cum)
    uv = each(lambda u_, v_: jnp.concatenate([u_, v_], axis=0), u, v)
    bkd = each(lambda b_, k_, g: jnp.concatenate([b_ * g, k_ * g], axis=0), b, k, g_end)
    s1 = each(lambda s, g, x, z: s * g[C - 1:C] + _dot_f32(x, z, ((0,), (0,))), s0, g_in, uv, bkd)
    for h in heads:
        st_scr[h] = s1[h]


def _rwkv7(p, mu, w0, w_w2, a0, w_a2, w_g2, k_k, k_a, r_k, ln_g, ln_b):
    B, S, _ = p.shape
    H, N = RWKV_HEADS, RWKV_HEAD
    prev = jnp.pad(p, ((0, 0), (1, 0), (0, 0)))[:, :-1]
    p = p + (prev - p) * mu
    r, k, v, w_lr, a_lr, g_lr = _split(p, RWKV_SIZES)
    log_w = -jax.nn.softplus(-(w0 + jnp.tanh(w_lr) @ w_w2)) - 0.5
    log_decay = -jnp.exp(log_w)
    a = jax.nn.sigmoid(a0 + a_lr @ w_a2)
    g = jax.nn.sigmoid(g_lr) @ w_g2
    kk = (k * k_k).reshape(B, S, H, N)
    kk = kk * lax.rsqrt(jnp.sum(kk * kk, axis=-1, keepdims=True) + 1e-12)
    k = k * (1.0 + (a - 1.0) * k_a)
    r_h, k_h, v_h, a_h, lw_h = [t.reshape(B, S, H, N) for t in (r, k, v, a, log_decay)]
    heads_first = lambda t: jnp.swapaxes(t, 1, 2)
    C = RWKV_CHUNK
    blk = pl.BlockSpec((1, H, C, N), lambda b, c: (b, 0, c, 0))
    y = pl.pallas_call(
        _rwkv_chunk_kernel,
        grid=(B, S // C),
        in_specs=[blk] * 6,
        out_specs=blk,
        out_shape=jax.ShapeDtypeStruct((B, H, S, N), F32),
        scratch_shapes=[pltpu.VMEM((H, N, N), F32)],
        compiler_params=pltpu.CompilerParams(dimension_semantics=("arbitrary", "arbitrary")),
        name="rwkv7_chunk",
    )(*[heads_first(t) for t in (r_h, lw_h, k_h, v_h, -kk, kk * a_h)])
    y = jnp.swapaxes(y, 1, 2)
    mean = jnp.mean(y, axis=-1, keepdims=True)
    var = jnp.mean(jnp.square(y - mean), axis=-1, keepdims=True)
    yn = ((y - mean) * lax.rsqrt(var + RWKV_LN_EPS)).reshape(B, S, H * N) * ln_g + ln_b
    bonus = (jnp.sum(r_h * k_h * r_k, axis=-1, keepdims=True) * v_h).reshape(B, S, H * N)
    return (yn + bonus) * g


def kernel(x, c, w_ada, b_ada, g_norm_mix, g_norm_ffn, w_in, gla_w_a2, gla_b_a, gla_g_o, dsa_g_qc, dsa_w_uq, dsa_w_iq, dsa_g_kv, dsa_w_uv, rwkv_mu, rwkv_w0, rwkv_w_w2, rwkv_a0, rwkv_w_a2, rwkv_w_g2, rwkv_k_k, rwkv_k_a, rwkv_r_k, rwkv_ln_g, rwkv_ln_b, w_branch, w_out, peer_w_q, peer_keys, peer_u, peer_v, g_final):
    B, S, D = x.shape
    depth = w_in.shape[0]
    in_sizes = _in_sizes(D)
    cond = jax.nn.silu(c)
    u_bf = peer_u.astype(BF16)
    vT_bf = jnp.swapaxes(peer_v, 1, 2).astype(BF16)
    for l in range(depth):
        mod = cond @ w_ada[l] + b_ada[l]
        sh_m, sc_m, gt_m, sh_f, sc_f, gt_f = [m[:, None, :] for m in jnp.split(mod, 6, axis=-1)]
        h = _rms(x, g_norm_mix[l]) * (1.0 + sc_m) + sh_m
        gq, gk, gv, galr, gr, qc, kvc, ik, iw, prw, gates = _split(h @ w_in[l], in_sizes)
        y_a = _gla(gq, gk, gv, galr, gr, gla_w_a2[l], gla_b_a[l], gla_g_o[l])
        y_b = _dsa(qc, kvc, ik, iw, dsa_g_qc[l], dsa_w_uq[l], dsa_w_iq[l], dsa_g_kv[l], dsa_w_uv[l])
        y_c = _rwkv7(prw, rwkv_mu[l], rwkv_w0[l], rwkv_w_w2[l], rwkv_a0[l], rwkv_w_a2[l], rwkv_w_g2[l],
                     rwkv_k_k[l], rwkv_k_a[l], rwkv_r_k[l], rwkv_ln_g[l], rwkv_ln_b[l])
        g_a, g_b, g_c = jnp.split(jax.nn.sigmoid(gates), N_BRANCH, axis=-1)
        merged = (g_a * (y_a @ w_branch[l, 0]) + g_b * (y_b @ w_branch[l, 1]) + g_c * (y_c @ w_branch[l, 2]))
        x = x + gt_m * (merged @ w_out[l])
        h2 = _rms(x, g_norm_ffn[l]) * (1.0 + sc_f) + sh_f
        x = x + gt_f * _peer(h2, peer_w_q[l], peer_keys[l], u_bf[l], vT_bf[l])
    return _rms(x, g_final)
```

```python
import functools

import jax
import jax.numpy as jnp
import numpy as np
from jax import lax
from jax.experimental import pallas as pl
from jax.experimental.pallas import tpu as pltpu

CHUNK = 64
Q_BLOCK = 128
NORM_EPS = 1e-6

GLA_HEADS, GLA_DK, GLA_DV, GLA_GATE_RANK, GLA_TAU = 4, 64, 128, 16, 16.0
DSA_HEADS, DSA_Q_RANK, DSA_LAT, DSA_DV = 8, 256, 128, 64
IDX_HEADS, IDX_DIM, DSA_TOPK_MAX = 4, 64, 256
RWKV_HEADS, RWKV_HEAD = 8, 64
RWKV_W_RANK, RWKV_A_RANK, RWKV_G_RANK = 64, 64, 128
RWKV_LN_EPS = 64e-5
RWKV_CHUNK = 64
RWKV_WR = RWKV_HEADS * RWKV_HEAD
RWKV_SIZES = (RWKV_WR, RWKV_WR, RWKV_WR, RWKV_W_RANK, RWKV_A_RANK, RWKV_G_RANK)
RWKV_WIDTH = sum(RWKV_SIZES)
N_BRANCH = 3
PEER_HEADS, PEER_NKEYS, PEER_QDIM, PEER_TOPK = 8, 128, 128, 16
PEER_N = PEER_NKEYS * PEER_NKEYS

LANES = 128
DSA_KEY_BLOCK = 512
INT_MIN = int(np.iinfo(np.int32).min)
LOG2_E = float(np.log2(np.e))
F32 = jnp.float32
BF16 = jnp.bfloat16
HIGHEST = lax.Precision.HIGHEST


def _in_sizes(d_model):
    return (GLA_HEADS * GLA_DK, GLA_HEADS * GLA_DK, GLA_HEADS * GLA_DV, GLA_GATE_RANK, GLA_HEADS * GLA_DV,
            DSA_Q_RANK, DSA_LAT, IDX_DIM, IDX_HEADS, RWKV_WIDTH, N_BRANCH * d_model)


def _rms(x, g, eps=NORM_EPS):
    xf = x.astype(F32)
    y = xf * lax.rsqrt(jnp.mean(xf * xf, axis=-1, keepdims=True) + eps)
    return (y * g).astype(x.dtype)


def _split(p, sizes):
    cuts = [int(s) for s in np.cumsum(sizes)[:-1]]
    return jnp.split(p, cuts, axis=-1)


def _split_bf16(x):
    hi = x.astype(BF16)
    return hi, (x - hi.astype(F32)).astype(BF16)


def _sortable_key(x):
    b = lax.bitcast_convert_type(x, jnp.int32)
    return b ^ ((b >> 31) & jnp.int32(0x7FFFFFFF))


def _dsa_kernel(q3_ref, wi_ref, q_ref, k3_ref, kvT_ref, kv_ref, wuv_ref, o_ref,
                key_scr, m_scr, l_scr, acc_scr, *, topk):
    QB, KB = Q_BLOCK, DSA_KEY_BLOCK
    bi = pl.program_id(1)
    nkb = (bi * QB + QB + KB - 1) // KB
    row = lax.broadcasted_iota(jnp.int32, (QB, KB), 0)
    lane = lax.broadcasted_iota(jnp.int32, (QB, KB), 1)
    q_end = ((bi * QB + row) // CHUNK + 1) * CHUNK
    wb = [jnp.broadcast_to(wi_ref[0, :, h:h + 1], (QB, KB)) for h in range(IDX_HEADS)]
    q3 = q3_ref[0].reshape(IDX_HEADS * QB, 3 * IDX_DIM)

    def key_off(kb):
        return pl.multiple_of(kb * KB, KB)

    def fold(x):
        return sum(x[:, g * LANES:(g + 1) * LANES] for g in range(1, KB // LANES)) + x[:, :LANES]

    def score_body(kb, c):
        s = jnp.dot(q3, k3_ref[0, :, pl.ds(key_off(kb), KB)], preferred_element_type=F32)
        sc = jnp.zeros((QB, KB), F32)
        for h in range(IDX_HEADS):
            sc = sc + jnp.maximum(s[h * QB:(h + 1) * QB], 0.0) * wb[h]
        key_scr[kb] = jnp.where(key_off(kb) + lane < q_end, _sortable_key(sc), INT_MIN)
        return c

    lax.fori_loop(0, nkb, score_body, 0)

    def count(pred, thr):
        def body(kb, acc):
            return acc + fold(pred(key_scr[kb], thr).astype(jnp.int32))
        acc = lax.fori_loop(0, nkb, body, jnp.zeros((QB, LANES), jnp.int32))
        return jnp.sum(acc, axis=1, keepdims=True)

    n_adm = q_end[:, 0:1]

    def bit_body(i, st):
        thr, cnt = st
        cand = thr + jnp.left_shift(jnp.int32(1), 31 - i)
        c = count(lambda k, t: k >= t, cand)
        take = c >= topk
        return jnp.where(take, cand, thr), jnp.where(take, c, cnt)

    thr, cnt = lax.fori_loop(0, 32, bit_body, (jnp.full((QB, 1), INT_MIN, jnp.int32), n_adm))

    def tie_path():
        n_gt = count(lambda k, t: k > t, thr)
        need = (topk - n_gt).astype(F32)
        r2 = lax.broadcasted_iota(jnp.int32, (KB, KB), 0)
        c2 = lax.broadcasted_iota(jnp.int32, (KB, KB), 1)
        tri = (r2 <= c2).astype(BF16)
        ones = jnp.ones((KB, KB), BF16)

        def mask_body(kb, carry):
            key = key_scr[kb]
            eq = key == thr
            eqb = eq.astype(BF16)
            pre = jnp.dot(eqb, tri, preferred_element_type=F32)
            sel = ((key > thr) | (eq & (carry + pre <= need))) & (key > INT_MIN)
            key_scr[kb] = jnp.where(sel, jnp.int32(1), jnp.int32(INT_MIN))
            return carry + jnp.dot(eqb, ones, preferred_element_type=F32)

        lax.fori_loop(0, nkb, mask_body, jnp.zeros((QB, KB), F32))
        return jnp.zeros((QB, 1), jnp.int32)

    thr_sel = lax.cond(jnp.max(jnp.where(cnt > topk, 1, 0)) > 0, tie_path,
                       lambda: jnp.maximum(thr, INT_MIN + 1))

    scale = DSA_LAT ** -0.5

    def logits(kb):
        return jnp.dot(q_ref[0, 0], kvT_ref[0, :, pl.ds(key_off(kb), KB)], preferred_element_type=F32)

    m_scr[...] = jnp.full(m_scr.shape, -jnp.inf, F32)

    def max_body(kb, c):
        sel = key_scr[kb] >= thr_sel
        lg = logits(kb)
        for h in range(DSA_HEADS):
            x = jnp.where(sel, lg[h * QB:(h + 1) * QB], -jnp.inf)
            xm = x[:, :LANES]
            for g in range(1, KB // LANES):
                xm = jnp.maximum(xm, x[:, g * LANES:(g + 1) * LANES])
            m_scr[h] = jnp.maximum(m_scr[h], xm)
        return c

    lax.fori_loop(0, nkb, max_body, 0)
    for h in range(DSA_HEADS):
        m_scr[h] = jnp.broadcast_to(jnp.max(m_scr[h], axis=1, keepdims=True), (QB, LANES))
    l_scr[...] = jnp.zeros(l_scr.shape, F32)
    acc_scr[...] = jnp.zeros(acc_scr.shape, F32)

    def acc_body(kb, c):
        sel = key_scr[kb] >= thr_sel
        kvb = kv_ref[0, pl.ds(key_off(kb), KB), :]
        lg = logits(kb)
        ps = []
        for h in range(DSA_HEADS):
            mb = jnp.concatenate([m_scr[h]] * (KB // LANES), axis=1)
            p = jnp.where(sel, jnp.exp2((lg[h * QB:(h + 1) * QB] - mb) * (scale * LOG2_E)), 0.0)
            l_scr[h] += fold(p)
            ps.append(p.astype(BF16))
        acc_scr[...] += jnp.dot(jnp.concatenate(ps, axis=0), kvb, preferred_element_type=F32)
        return c

    lax.fori_loop(0, nkb, acc_body, 0)
    for h in range(DSA_HEADS):
        o = acc_scr[h * QB:(h + 1) * QB] / jnp.sum(l_scr[h], axis=1, keepdims=True)
        o_ref[0, :, h * DSA_DV:(h + 1) * DSA_DV] = jnp.dot(
            o.astype(BF16), wuv_ref[h], preferred_element_type=F32)


def _dsa(qc, kvc, idx_k, idx_w, g_qc, w_uq, w_iq, g_kv, w_uv):
    B, S, _ = kvc.shape
    topk = min(DSA_TOPK_MAX, S // 4)
    nb = S // Q_BLOCK
    qcn = _rms(qc, g_qc)
    q = jnp.dot(qcn, w_uq).astype(BF16)
    q_rows = q.reshape(B, nb, Q_BLOCK, DSA_HEADS, DSA_LAT).transpose(0, 1, 3, 2, 4).reshape(
        B, nb, DSA_HEADS * Q_BLOCK, DSA_LAT)
    qi = jnp.dot(qcn, w_iq, precision=HIGHEST)
    qi_t = qi.reshape(B, S, IDX_HEADS, IDX_DIM).transpose(0, 2, 1, 3)
    q_hi, q_lo = _split_bf16(qi_t)
    q3 = jnp.concatenate([q_hi, q_lo, q_hi], axis=-1)
    k_hi, k_lo = _split_bf16(idx_k.transpose(0, 2, 1))
    k3 = jnp.concatenate([k_hi, k_hi, k_lo], axis=1)
    kv = _rms(kvc, g_kv).astype(BF16)
    kvT = kv.transpose(0, 2, 1)
    return pl.pallas_call(
        functools.partial(_dsa_kernel, topk=topk),
        grid=(B, nb),
        in_specs=[
            pl.BlockSpec((1, IDX_HEADS, Q_BLOCK, 3 * IDX_DIM), lambda b, i: (b, 0, i, 0)),
            pl.BlockSpec((1, Q_BLOCK, IDX_HEADS), lambda b, i: (b, i, 0)),
            pl.BlockSpec((1, 1, DSA_HEADS * Q_BLOCK, DSA_LAT), lambda b, i: (b, i, 0, 0)),
            pl.BlockSpec((1, 3 * IDX_DIM, S), lambda b, i: (b, 0, 0)),
            pl.BlockSpec((1, DSA_LAT, S), lambda b, i: (b, 0, 0)),
            pl.BlockSpec((1, S, DSA_LAT), lambda b, i: (b, 0, 0)),
            pl.BlockSpec((DSA_HEADS, DSA_LAT, DSA_DV), lambda b, i: (0, 0, 0)),
        ],
        out_specs=pl.BlockSpec((1, Q_BLOCK, DSA_HEADS * DSA_DV), lambda b, i: (b, i, 0)),
        out_shape=jax.ShapeDtypeStruct((B, S, DSA_HEADS * DSA_DV), F32),
        scratch_shapes=[
            pltpu.VMEM((S // DSA_KEY_BLOCK, Q_BLOCK, DSA_KEY_BLOCK), jnp.int32),
            pltpu.VMEM((DSA_HEADS, Q_BLOCK, LANES), F32),
            pltpu.VMEM((DSA_HEADS, Q_BLOCK, LANES), F32),
            pltpu.VMEM((DSA_HEADS * Q_BLOCK, DSA_LAT), F32),
        ],
        compiler_params=pltpu.CompilerParams(
            dimension_semantics=("arbitrary", "arbitrary"),
            vmem_limit_bytes=48 * 1024 * 1024),
        name="dsa_fused",
    )(q3, idx_w, q_rows, k3, kvT, kv, w_uv.astype(BF16))


def _extract_topk(s, k):
    R = s.shape[0]
    rid = lax.broadcasted_iota(jnp.int32, s.shape, 0)
    rows = []
    cur = s
    for _ in range(k):
        m = jnp.max(cur, axis=0, keepdims=True)
        rows.append(m)
        first = jnp.min(jnp.where(cur == m, rid, R), axis=0, keepdims=True)
        cur = jnp.where(rid == first, -jnp.inf, cur)
    return jnp.concatenate(rows, axis=0)


def _peer_route_kernel(hT_ref, wqT_ref, keys_ref, s1_ref, s2_ref, tau_ref):
    half = PEER_QDIM // 2
    half_k = PEER_TOPK // 2

    def pair_sums(a, b):
        return jnp.concatenate(
            [a + b[0:1]] + [a[:half_k] + b[j:j + 1] for j in range(1, half_k)] + [a[0:1] + b[half_k:]], axis=0)

    def kth(c):
        return _extract_topk(c, PEER_TOPK)[PEER_TOPK - 1:PEER_TOPK]

    qT = jnp.dot(wqT_ref[...], hT_ref[...], precision=HIGHEST, preferred_element_type=F32)
    for h in range(PEER_HEADS):
        s = []
        for p in range(2):
            r0 = (h * 2 + p) * half
            s.append(jnp.dot(keys_ref[h * 2 + p], qT[r0:r0 + half], precision=HIGHEST,
                             preferred_element_type=F32))
        a = _extract_topk(s[0], PEER_TOPK)
        b = _extract_topk(s[1], PEER_TOPK)
        c = pair_sums(a, b)
        m = a[0:1] + b[0:1]
        sel = c >= kth(c)
        z = jnp.sum(jnp.where(sel, jnp.exp2(c - m), 0.0), axis=0, keepdims=True)
        shift = m + jnp.log2(z)
        s1_ref[h] = s[0] - shift
        s2_ref[h] = s[1]
        tau_ref[h] = jnp.min(jnp.where(sel, pair_sums(a - shift, b), jnp.inf), axis=0, keepdims=True)


def _peer_dense_kernel(hT_ref, u_ref, vT_ref, s1_ref, s2_ref, tau_ref, o_ref, a_scr, gw_scr, *, eb_rows):
    eb = pl.program_id(1)
    TT = hT_ref.shape[1]

    @pl.when(eb == 0)
    def _():
        o_ref[...] = jnp.zeros(o_ref.shape, F32)

    a_scr[...] = jnp.dot(u_ref[...], hT_ref[...], preferred_element_type=F32)
    JH = PEER_NKEYS // 2
    for ts in range(TT // LANES):
        tsl = slice(ts * LANES, (ts + 1) * LANES)
        for jh in range(2):
            jsl = slice(jh * JH, (jh + 1) * JH)
            accs = [jnp.zeros((JH, LANES), F32) for _ in range(eb_rows)]
            for h in range(PEER_HEADS):
                s2t, taut = s2_ref[h, jsl, tsl], tau_ref[h, :, tsl]
                for ii in range(eb_rows):
                    i = eb * eb_rows + ii
                    c = s2t + s1_ref[h, pl.ds(i, 1), :][:, tsl]
                    accs[ii] = accs[ii] + jnp.where(c >= taut, jnp.exp2(c), 0.0)
            for ii in range(eb_rows):
                rsl = slice(ii * PEER_NKEYS + jh * JH, ii * PEER_NKEYS + (jh + 1) * JH)
                a = a_scr[rsl, tsl]
                g = 0.5 * a * (1.0 + lax.erf(a * (2.0 ** -0.5)))
                gw_scr[rsl, tsl] = (accs[ii] * g).astype(BF16)
    o_ref[...] += jnp.dot(vT_ref[...], gw_scr[...], preferred_element_type=F32)


def _peer(h2, w_q, sub_keys, u_bf, vT_bf):
    B, S, D = h2.shape
    N = B * S
    TT = 256 if N % 256 == 0 else LANES
    EB = 512
    hT = h2.reshape(N, D).T
    keys = sub_keys.reshape(PEER_HEADS * 2, PEER_NKEYS, PEER_QDIM // 2) * LOG2_E
    stat = jax.ShapeDtypeStruct((PEER_HEADS, PEER_NKEYS, N), F32)
    stat_spec = pl.BlockSpec((PEER_HEADS, PEER_NKEYS, TT), lambda t: (0, 0, t))
    s1, s2, tau = pl.pallas_call(
        _peer_route_kernel,
        grid=(N // TT,),
        in_specs=[
            pl.BlockSpec((D, TT), lambda t: (0, t)),
            pl.BlockSpec((PEER_HEADS * PEER_QDIM, D), lambda t: (0, 0)),
            pl.BlockSpec((PEER_HEADS * 2, PEER_NKEYS, PEER_QDIM // 2), lambda t: (0, 0, 0)),
        ],
        out_specs=[stat_spec, stat_spec, pl.BlockSpec((PEER_HEADS, 1, TT), lambda t: (0, 0, t))],
        out_shape=[stat, stat, jax.ShapeDtypeStruct((PEER_HEADS, 1, N), F32)],
        compiler_params=pltpu.CompilerParams(
            dimension_semantics=("arbitrary",), vmem_limit_bytes=48 * 1024 * 1024),
        name="peer_route",
    )(hT, w_q.T, keys)

    TD = 1024 if N % 1024 == 0 else TT
    stat_spec2 = pl.BlockSpec((PEER_HEADS, PEER_NKEYS, TD), lambda t, e: (0, 0, t))
    oT = pl.pallas_call(
        functools.partial(_peer_dense_kernel, eb_rows=EB // PEER_NKEYS),
        grid=(N // TD, PEER_N // EB),
        in_specs=[
            pl.BlockSpec((D, TD), lambda t, e: (0, t)),
            pl.BlockSpec((EB, D), lambda t, e: (e, 0)),
            pl.BlockSpec((D, EB), lambda t, e: (0, e)),
            stat_spec2, stat_spec2,
            pl.BlockSpec((PEER_HEADS, 1, TD), lambda t, e: (0, 0, t)),
        ],
        out_specs=pl.BlockSpec((D, TD), lambda t, e: (0, t)),
        out_shape=jax.ShapeDtypeStruct((D, N), F32),
        scratch_shapes=[pltpu.VMEM((EB, TD), F32), pltpu.VMEM((EB, TD), BF16)],
        compiler_params=pltpu.CompilerParams(
            dimension_semantics=("arbitrary", "arbitrary"), vmem_limit_bytes=56 * 1024 * 1024),
        name="peer_dense",
    )(hT.astype(BF16), u_bf, vT_bf, s1, s2, tau)
    return oT.T.reshape(B, S, D)


def _gla(q, k, v, a_lr, r, w_a2, b_a, g_o):
    B, S, _ = q.shape
    nc = S // CHUNK
    H, dk, dv = GLA_HEADS, GLA_DK, GLA_DV
    q = q.reshape(B, nc, CHUNK, H, dk) * (dk ** -0.5)
    k = k.reshape(B, nc, CHUNK, H, dk)
    v = v.reshape(B, nc, CHUNK, H, dv)
    log_a = jax.nn.log_sigmoid(a_lr @ w_a2 + b_a) / GLA_TAU
    log_a = log_a.reshape(B, nc, CHUNK, H, dk)
    cum = jnp.cumsum(log_a, axis=2)
    tot = cum[:, :, -1]
    k_dec = k * jnp.exp(tot[:, :, None] - cum)
    upd = jnp.einsum('bnchk,bnchv->nbhkv', k_dec, v)
    chunk_decay = jnp.exp(tot).transpose(1, 0, 2, 3)

    def step(state, inp):
        d, u = inp
        state = d[..., None] * state + u
        return state, state

    _, states = lax.scan(step, jnp.zeros((B, H, dk, dv), F32), (chunk_decay, upd))
    o = jnp.einsum('bnchk,nbhkv->bnchv', q, states).reshape(B, S, H, dv)
    return _rms(o, g_o).reshape(B, S, H * dv) * jax.nn.silu(r)


def _dot_f32(x, y, dims=((1,), (0,))):
    xh, xl = x if isinstance(x, tuple) else _split_bf16(x)
    yh, yl = y if isinstance(y, tuple) else _split_bf16(y)
    mm = lambda p, q: lax.dot_general(p, q, (dims, ((), ())), preferred_element_type=F32)
    return mm(xh, yh) + (mm(xl, yh) + mm(xh, yl))


def _cumsum_rows(ltri_bf, x):
    h1, r1 = _split_bf16(x)
    h2 = r1.astype(F32)
    r2 = x - h1.astype(F32) - h2
    mm = lambda q: jnp.dot(ltri_bf, q, preferred_element_type=F32)
    return mm(h1) + (mm(r1) + mm(r2.astype(BF16)))


def _rwkv_chunk_kernel(r_ref, lw_ref, k_ref, v_ref, a_ref, b_ref, y_ref, st_scr):
    H, C = r_ref.shape[1], r_ref.shape[2]

    @pl.when(pl.program_id(1) == 0)
    def _():
        st_scr[...] = jnp.zeros(st_scr.shape, F32)

    row = lax.broadcasted_iota(jnp.int32, (C, C), 0)
    col = lax.broadcasted_iota(jnp.int32, (C, C), 1)
    lower = col <= row
    strict = col < row
    ltri = lower.astype(BF16)
    eye = (row == col).astype(F32)
    heads = range(H)
    each = lambda f, *xs: [f(*t) for t in zip(*xs)]
    nt = ((1,), (1,))
    lw, r, k, v, a, b = ([ref[0, h] for h in heads] for ref in (lw_ref, r_ref, k_ref, v_ref, a_ref, b_ref))
    s0 = [st_scr[h] for h in heads]
    cum = each(lambda x: _cumsum_rows(ltri, x), lw)
    g_in = each(jnp.exp, cum)
    g_inv = each(lambda c: jnp.exp(-c), cum)
    ar = each(lambda a_, r_, c, l, g: _split_bf16(jnp.concatenate([a_ * jnp.exp(c - l), r_ * g], axis=0)),
              a, r, cum, lw, g_in)
    bk = each(lambda b_, k_, g: jnp.concatenate([b_ * g, k_ * g], axis=0), b, k, g_inv)
    v_s = each(_split_bf16, v)
    p = each(lambda x, y: _dot_f32(x, y, nt), ar, bk)
    ars = each(lambda x, s: _dot_f32(x, s, nt), ar, s0)
    ab = each(lambda q: jnp.where(strict, q[:C, :C], 0.0), p)
    ak = each(lambda q: jnp.where(strict, q[:C, C:], 0.0), p)
    rb = each(lambda q: jnp.where(lower, q[C:, :C], 0.0), p)
    rk = each(lambda q: jnp.where(lower, q[C:, C:], 0.0), p)
    rhs = each(lambda x, m, vs: x[:C] + _dot_f32(m, vs), ars, ak, v_s)
    inv, pw = each(lambda m: eye + m, ab), ab
    for _ in range(int(np.log2(C)) - 1):
        pw = each(lambda m: (lambda s: _dot_f32(s, s))(_split_bf16(m)), pw)
        inv = each(lambda i_, m: i_ + _dot_f32(i_, m), inv, pw)
    u = each(_dot_f32, inv, rhs)
    y = each(lambda x, m1, u_, m2, vs: x[C:] + _dot_f32(m1, u_) + _dot_f32(m2, vs), ars, rb, u, rk, v_s)
    for h in heads:
        y_ref[0, h] = y[h]
    g_end = each(lambda c: jnp.exp(c[C - 1:C] - c), cum)
    uv = each(lambda u_, v_: jnp.concatenate([u_, v_], axis=0), u, v)
    bkd = each(lambda b_, k_, g: jnp.concatenate([b_ * g, k_ * g], axis=0), b, k, g_end)
    s1 = each(lambda s, g, x, z: s * g[C - 1:C] + _dot_f32(x, z, ((0,), (0,))), s0, g_in, uv, bkd)
    for h in heads:
        st_scr[h] = s1[h]


def _rwkv7(p, mu, w0, w_w2, a0, w_a2, w_g2, k_k, k_a, r_k, ln_g, ln_b):
    B, S, _ = p.shape
    H, N = RWKV_HEADS, RWKV_HEAD
    prev = jnp.pad(p, ((0, 0), (1, 0), (0, 0)))[:, :-1]
    p = p + (prev - p) * mu
    r, k, v, w_lr, a_lr, g_lr = _split(p, RWKV_SIZES)
    log_w = -jax.nn.softplus(-(w0 + jnp.tanh(w_lr) @ w_w2)) - 0.5
    log_decay = -jnp.exp(log_w)
    a = jax.nn.sigmoid(a0 + a_lr @ w_a2)
    g = jax.nn.sigmoid(g_lr) @ w_g2
    kk = (k * k_k).reshape(B, S, H, N)
    kk = kk * lax.rsqrt(jnp.sum(kk * kk, axis=-1, keepdims=True) + 1e-12)
    k = k * (1.0 + (a - 1.0) * k_a)
    r_h, k_h, v_h, a_h, lw_h = [t.reshape(B, S, H, N) for t in (r, k, v, a, log_decay)]
    heads_first = lambda t: jnp.swapaxes(t, 1, 2)
    C = RWKV_CHUNK
    blk = pl.BlockSpec((1, H, C, N), lambda b, c: (b, 0, c, 0))
    y = pl.pallas_call(
        _rwkv_chunk_kernel,
        grid=(B, S // C),
        in_specs=[blk] * 6,
        out_specs=blk,
        out_shape=jax.ShapeDtypeStruct((B, H, S, N), F32),
        scratch_shapes=[pltpu.VMEM((H, N, N), F32)],
        compiler_params=pltpu.CompilerParams(dimension_semantics=("arbitrary", "arbitrary")),
        name="rwkv7_chunk",
    )(*[heads_first(t) for t in (r_h, lw_h, k_h, v_h, -kk, kk * a_h)])
    y = jnp.swapaxes(y, 1, 2)
    mean = jnp.mean(y, axis=-1, keepdims=True)
    var = jnp.mean(jnp.square(y - mean), axis=-1, keepdims=True)
    yn = ((y - mean) * lax.rsqrt(var + RWKV_LN_EPS)).reshape(B, S, H * N) * ln_g + ln_b
    bonus = (jnp.sum(r_h * k_h * r_k, axis=-1, keepdims=True) * v_h).reshape(B, S, H * N)
    return (yn + bonus) * g


def kernel(x, c, w_ada, b_ada, g_norm_mix, g_norm_ffn, w_in, gla_w_a2, gla_b_a, gla_g_o, dsa_g_qc, dsa_w_uq, dsa_w_iq, dsa_g_kv, dsa_w_uv, rwkv_mu, rwkv_w0, rwkv_w_w2, rwkv_a0, rwkv_w_a2, rwkv_w_g2, rwkv_k_k, rwkv_k_a, rwkv_r_k, rwkv_ln_g, rwkv_ln_b, w_branch, w_out, peer_w_q, peer_keys, peer_u, peer_v, g_final):
    B, S, D = x.shape
    depth = w_in.shape[0]
    in_sizes = _in_sizes(D)
    cond = jax.nn.silu(c)
    u_bf = peer_u.astype(BF16)
    vT_bf = jnp.swapaxes(peer_v, 1, 2).astype(BF16)
    for l in range(depth):
        mod = cond @ w_ada[l] + b_ada[l]
        sh_m, sc_m, gt_m, sh_f, sc_f, gt_f = [m[:, None, :] for m in jnp.split(mod, 6, axis=-1)]
        h = _rms(x, g_norm_mix[l]) * (1.0 + sc_m) + sh_m
        gq, gk, gv, galr, gr, qc, kvc, ik, iw, prw, gates = _split(h @ w_in[l], in_sizes)
        y_a = _gla(gq, gk, gv, galr, gr, gla_w_a2[l], gla_b_a[l], gla_g_o[l])
        y_b = _dsa(qc, kvc, ik, iw, dsa_g_qc[l], dsa_w_uq[l], dsa_w_iq[l], dsa_g_kv[l], dsa_w_uv[l])
        y_c = _rwkv7(prw, rwkv_mu[l], rwkv_w0[l], rwkv_w_w2[l], rwkv_a0[l], rwkv_w_a2[l], rwkv_w_g2[l],
                     rwkv_k_k[l], rwkv_k_a[l], rwkv_r_k[l], rwkv_ln_g[l], rwkv_ln_b[l])
        g_a, g_b, g_c = jnp.split(jax.nn.sigmoid(gates), N_BRANCH, axis=-1)
        merged = (g_a * (y_a @ w_branch[l, 0]) + g_b * (y_b @ w_branch[l, 1]) + g_c * (y_c @ w_branch[l, 2]))
        x = x + gt_m * (merged @ w_out[l])
        h2 = _rms(x, g_norm_ffn[l]) * (1.0 + sc_f) + sh_f
        x = x + gt_f * _peer(h2, peer_w_q[l], peer_keys[l], u_bf[l], vT_bf[l])
    return _rms(x, g_final)
```

```python
import functools

import jax
import jax.numpy as jnp
import numpy as np
from jax import lax
from jax.experimental import pallas as pl
from jax.experimental.pallas import tpu as pltpu

CHUNK = 64
Q_BLOCK = 128
NORM_EPS = 1e-6

GLA_HEADS, GLA_DK, GLA_DV, GLA_GATE_RANK, GLA_TAU = 4, 64, 128, 16, 16.0
DSA_HEADS, DSA_Q_RANK, DSA_LAT, DSA_DV = 8, 256, 128, 64
IDX_HEADS, IDX_DIM, DSA_TOPK_MAX = 4, 64, 256
RWKV_HEADS, RWKV_HEAD = 8, 64
RWKV_W_RANK, RWKV_A_RANK, RWKV_G_RANK = 64, 64, 128
RWKV_LN_EPS = 64e-5
RWKV_CHUNK = 64
RWKV_WR = RWKV_HEADS * RWKV_HEAD
RWKV_SIZES = (RWKV_WR, RWKV_WR, RWKV_WR, RWKV_W_RANK, RWKV_A_RANK, RWKV_G_RANK)
RWKV_WIDTH = sum(RWKV_SIZES)
N_BRANCH = 3
PEER_HEADS, PEER_NKEYS, PEER_QDIM, PEER_TOPK = 8, 128, 128, 16
PEER_N = PEER_NKEYS * PEER_NKEYS

LANES = 128
DSA_KEY_BLOCK = 512
INT_MIN = int(np.iinfo(np.int32).min)
LOG2_E = float(np.log2(np.e))
F32 = jnp.float32
BF16 = jnp.bfloat16
HIGHEST = lax.Precision.HIGHEST


def _in_sizes(d_model):
    return (GLA_HEADS * GLA_DK, GLA_HEADS * GLA_DK, GLA_HEADS * GLA_DV, GLA_GATE_RANK, GLA_HEADS * GLA_DV,
            DSA_Q_RANK, DSA_LAT, IDX_DIM, IDX_HEADS, RWKV_WIDTH, N_BRANCH * d_model)


def _rms(x, g, eps=NORM_EPS):
    xf = x.astype(F32)
    y = xf * lax.rsqrt(jnp.mean(xf * xf, axis=-1, keepdims=True) + eps)
    return (y * g).astype(x.dtype)


def _split(p, sizes):
    cuts = [int(s) for s in np.cumsum(sizes)[:-1]]
    return jnp.split(p, cuts, axis=-1)


def _split_bf16(x):
    hi = x.astype(BF16)
    return hi, (x - hi.astype(F32)).astype(BF16)


def _sortable_key(x):
    b = lax.bitcast_convert_type(x, jnp.int32)
    return b ^ ((b >> 31) & jnp.int32(0x7FFFFFFF))


def _dsa_kernel(q3_ref, wi_ref, q_ref, k3_ref, kvT_ref, kv_ref, wuv_ref, o_ref,
                key_scr, m_scr, l_scr, acc_scr, *, topk):
    QB, KB = Q_BLOCK, DSA_KEY_BLOCK
    bi = pl.program_id(1)
    nkb = (bi * QB + QB + KB - 1) // KB
    row = lax.broadcasted_iota(jnp.int32, (QB, KB), 0)
    lane = lax.broadcasted_iota(jnp.int32, (QB, KB), 1)
    q_end = ((bi * QB + row) // CHUNK + 1) * CHUNK
    wb = [jnp.broadcast_to(wi_ref[0, :, h:h + 1], (QB, KB)) for h in range(IDX_HEADS)]
    q3 = q3_ref[0].reshape(IDX_HEADS * QB, 3 * IDX_DIM)

    def key_off(kb):
        return pl.multiple_of(kb * KB, KB)

    def fold(x):
        return sum(x[:, g * LANES:(g + 1) * LANES] for g in range(1, KB // LANES)) + x[:, :LANES]

    def score_body(kb, c):
        s = jnp.dot(q3, k3_ref[0, :, pl.ds(key_off(kb), KB)], preferred_element_type=F32)
        sc = jnp.zeros((QB, KB), F32)
        for h in range(IDX_HEADS):
            sc = sc + jnp.maximum(s[h * QB:(h + 1) * QB], 0.0) * wb[h]
        key_scr[kb] = jnp.where(key_off(kb) + lane < q_end, _sortable_key(sc), INT_MIN)
        return c

    lax.fori_loop(0, nkb, score_body, 0)

    def count(pred, thr):
        def body(kb, acc):
            return acc + fold(pred(key_scr[kb], thr).astype(jnp.int32))
        acc = lax.fori_loop(0, nkb, body, jnp.zeros((QB, LANES), jnp.int32))
        return jnp.sum(acc, axis=1, keepdims=True)

    n_adm = q_end[:, 0:1]

    def bit_body(i, st):
        thr, cnt = st
        cand = thr + jnp.left_shift(jnp.int32(1), 31 - i)
        c = count(lambda k, t: k >= t, cand)
        take = c >= topk
        return jnp.where(take, cand, thr), jnp.where(take, c, cnt)

    thr, cnt = lax.fori_loop(0, 32, bit_body, (jnp.full((QB, 1), INT_MIN, jnp.int32), n_adm))

    def tie_path():
        n_gt = count(lambda k, t: k > t, thr)
        need = (topk - n_gt).astype(F32)
        r2 = lax.broadcasted_iota(jnp.int32, (KB, KB), 0)
        c2 = lax.broadcasted_iota(jnp.int32, (KB, KB), 1)
        tri = (r2 <= c2).astype(BF16)
        ones = jnp.ones((KB, KB), BF16)

        def mask_body(kb, carry):
            key = key_scr[kb]
            eq = key == thr
            eqb = eq.astype(BF16)
            pre = jnp.dot(eqb, tri, preferred_element_type=F32)
            sel = ((key > thr) | (eq & (carry + pre <= need))) & (key > INT_MIN)
            key_scr[kb] = jnp.where(sel, jnp.int32(1), jnp.int32(INT_MIN))
            return carry + jnp.dot(eqb, ones, preferred_element_type=F32)

        lax.fori_loop(0, nkb, mask_body, jnp.zeros((QB, KB), F32))
        return jnp.zeros((QB, 1), jnp.int32)

    thr_sel = lax.cond(jnp.max(jnp.where(cnt > topk, 1, 0)) > 0, tie_path,
                       lambda: jnp.maximum(thr, INT_MIN + 1))

    scale = DSA_LAT ** -0.5

    def logits(kb):
        return jnp.dot(q_ref[0, 0], kvT_ref[0, :, pl.ds(key_off(kb), KB)], preferred_element_type=F32)

    m_scr[...] = jnp.full(m_scr.shape, -jnp.inf, F32)

    def max_body(kb, c):
        sel = key_scr[kb] >= thr_sel
        lg = logits(kb)
        for h in range(DSA_HEADS):
            x = jnp.where(sel, lg[h * QB:(h + 1) * QB], -jnp.inf)
            xm = x[:, :LANES]
            for g in range(1, KB // LANES):
                xm = jnp.maximum(xm, x[:, g * LANES:(g + 1) * LANES])
            m_scr[h] = jnp.maximum(m_scr[h], xm)
        return c

    lax.fori_loop(0, nkb, max_body, 0)
    for h in range(DSA_HEADS):
        m_scr[h] = jnp.broadcast_to(jnp.max(m_scr[h], axis=1, keepdims=True), (QB, LANES))
    l_scr[...] = jnp.zeros(l_scr.shape, F32)
    acc_scr[...] = jnp.zeros(acc_scr.shape, F32)

    def acc_body(kb, c):
        sel = key_scr[kb] >= thr_sel
        kvb = kv_ref[0, pl.ds(key_off(kb), KB), :]
        lg = logits(kb)
        ps = []
        for h in range(DSA_HEADS):
            mb = jnp.concatenate([m_scr[h]] * (KB // LANES), axis=1)
            p = jnp.where(sel, jnp.exp2((lg[h * QB:(h + 1) * QB] - mb) * (scale * LOG2_E)), 0.0)
            l_scr[h] += fold(p)
            ps.append(p.astype(BF16))
        acc_scr[...] += jnp.dot(jnp.concatenate(ps, axis=0), kvb, preferred_element_type=F32)
        return c

    lax.fori_loop(0, nkb, acc_body, 0)
    for h in range(DSA_HEADS):
        o = acc_scr[h * QB:(h + 1) * QB] / jnp.sum(l_scr[h], axis=1, keepdims=True)
        o_ref[0, :, h * DSA_DV:(h + 1) * DSA_DV] = jnp.dot(
            o.astype(BF16), wuv_ref[h], preferred_element_type=F32)


def _dsa(qc, kvc, idx_k, idx_w, g_qc, w_uq, w_iq, g_kv, w_uv):
    B, S, _ = kvc.shape
    topk = min(DSA_TOPK_MAX, S // 4)
    nb = S // Q_BLOCK
    qcn = _rms(qc, g_qc)
    q = jnp.dot(qcn, w_uq).astype(BF16)
    q_rows = q.reshape(B, nb, Q_BLOCK, DSA_HEADS, DSA_LAT).transpose(0, 1, 3, 2, 4).reshape(
        B, nb, DSA_HEADS * Q_BLOCK, DSA_LAT)
    qi = jnp.dot(qcn, w_iq, precision=HIGHEST)
    qi_t = qi.reshape(B, S, IDX_HEADS, IDX_DIM).transpose(0, 2, 1, 3)
    q_hi, q_lo = _split_bf16(qi_t)
    q3 = jnp.concatenate([q_hi, q_lo, q_hi], axis=-1)
    k_hi, k_lo = _split_bf16(idx_k.transpose(0, 2, 1))
    k3 = jnp.concatenate([k_hi, k_hi, k_lo], axis=1)
    kv = _rms(kvc, g_kv).astype(BF16)
    kvT = kv.transpose(0, 2, 1)
    return pl.pallas_call(
        functools.partial(_dsa_kernel, topk=topk),
        grid=(B, nb),
        in_specs=[
            pl.BlockSpec((1, IDX_HEADS, Q_BLOCK, 3 * IDX_DIM), lambda b, i: (b, 0, i, 0)),
            pl.BlockSpec((1, Q_BLOCK, IDX_HEADS), lambda b, i: (b, i, 0)),
            pl.BlockSpec((1, 1, DSA_HEADS * Q_BLOCK, DSA_LAT), lambda b, i: (b, i, 0, 0)),
            pl.BlockSpec((1, 3 * IDX_DIM, S), lambda b, i: (b, 0, 0)),
            pl.BlockSpec((1, DSA_LAT, S), lambda b, i: (b, 0, 0)),
            pl.BlockSpec((1, S, DSA_LAT), lambda b, i: (b, 0, 0)),
            pl.BlockSpec((DSA_HEADS, DSA_LAT, DSA_DV), lambda b, i: (0, 0, 0)),
        ],
        out_specs=pl.BlockSpec((1, Q_BLOCK, DSA_HEADS * DSA_DV), lambda b, i: (b, i, 0)),
        out_shape=jax.ShapeDtypeStruct((B, S, DSA_HEADS * DSA_DV), F32),
        scratch_shapes=[
            pltpu.VMEM((S // DSA_KEY_BLOCK, Q_BLOCK, DSA_KEY_BLOCK), jnp.int32),
            pltpu.VMEM((DSA_HEADS, Q_BLOCK, LANES), F32),
            pltpu.VMEM((DSA_HEADS, Q_BLOCK, LANES), F32),
            pltpu.VMEM((DSA_HEADS * Q_BLOCK, DSA_LAT), F32),
        ],
        compiler_params=pltpu.CompilerParams(
            dimension_semantics=("arbitrary", "arbitrary"),
            vmem_limit_bytes=48 * 1024 * 1024),
        name="dsa_fused",
    )(q3, idx_w, q_rows, k3, kvT, kv, w_uv.astype(BF16))


SUBLANES = 8


def _merge_exchange_pairs(n):
    t = int(np.ceil(np.log2(n)))
    p, pairs = 1 << (t - 1), []
    while p > 0:
        q, r, d = 1 << (t - 1), 0, p
        while d > 0:
            pairs += [(i, i + d) for i in range(n - d) if (i & p) == r]
            d, q, r = q - p, q >> 1, p
        p >>= 1
    return pairs


def _exchange(rows, pairs):
    rows = list(rows)
    for i, j in pairs:
        rows[i], rows[j] = jnp.maximum(rows[i], rows[j]), jnp.minimum(rows[i], rows[j])
    return rows


def _merge_sublane_lists(rows):
    k = len(rows)
    clean = [(i, i + d) for d in (k >> s for s in range(1, k.bit_length())) for i in range(k) if (i & d) == 0]
    for sh in (4, 2, 1):
        far = [pltpu.roll(r, sh, axis=0) for r in rows]
        rows = _exchange([jnp.maximum(rows[i], far[k - 1 - i]) for i in range(k)], clean)
    return rows


def _top_sorted(s, k):
    rows = [s[SUBLANES * v:SUBLANES * (v + 1)] for v in range(s.shape[0] // SUBLANES)]
    assert len(rows) == k
    return _merge_sublane_lists(_exchange(rows, _merge_exchange_pairs(k)))


def _peer_route_kernel(hT_ref, wqT_ref, keys_ref, s1_ref, s2_ref, tau_ref):
    half = PEER_QDIM // 2
    K = PEER_TOPK
    TT = hT_ref.shape[1]
    sub = lax.broadcasted_iota(jnp.int32, (SUBLANES, TT), 0)

    def spread(rows):
        out = rows[SUBLANES - 1]
        for j in range(SUBLANES - 2, -1, -1):
            out = jnp.where(sub == j, rows[j], out)
        return out

    qT = jnp.dot(wqT_ref[...], hT_ref[...], precision=HIGHEST, preferred_element_type=F32)
    for h in range(PEER_HEADS):
        s = []
        for p in range(2):
            r0 = (h * 2 + p) * half
            s.append(jnp.dot(keys_ref[h * 2 + p], qT[r0:r0 + half], precision=HIGHEST,
                             preferred_element_type=F32))
        a = _top_sorted(s[0], K)
        b = _top_sorted(s[1], K)
        b_lo, b_hi = spread(b[:SUBLANES]), spread(b[SUBLANES:])
        c = [a[i] + b_lo for i in range(K)]
        x = a[0] + b_hi
        for i in range(K):
            c[i], x = jnp.maximum(c[i], x), jnp.minimum(c[i], x)
        top = _merge_sublane_lists(c)
        tau = top[K - 1]
        z = sum(jnp.exp2(t - top[0]) for t in top)
        shift = top[0] + jnp.log2(z)
        s1_ref[h] = s[0] - shift[0:1]
        s2_ref[h] = s[1]
        low = jnp.where(a[0] + b_hi >= tau, (a[0] - shift) + b_hi, jnp.inf)
        for i in range(K):
            low = jnp.minimum(low, jnp.where(a[i] + b_lo >= tau, (a[i] - shift) + b_lo, jnp.inf))
        tau_ref[h] = jnp.min(low, axis=0, keepdims=True)


def _peer_dense_kernel(hT_ref, u_ref, vT_ref, s1_ref, s2_ref, tau_ref, o_ref, a_scr, gw_scr, *, eb_rows):
    eb = pl.program_id(1)
    TT = hT_ref.shape[1]

    @pl.when(eb == 0)
    def _():
        o_ref[...] = jnp.zeros(o_ref.shape, F32)

    a_scr[...] = jnp.dot(u_ref[...], hT_ref[...], preferred_element_type=F32)
    JH = PEER_NKEYS // 4
    for ts in range(TT // LANES):
        tsl = slice(ts * LANES, (ts + 1) * LANES)
        for jh in range(PEER_NKEYS // JH):
            jsl = slice(jh * JH, (jh + 1) * JH)
            accs = [jnp.zeros((JH, LANES), F32) for _ in range(eb_rows)]
            for h in range(PEER_HEADS):
                s2t, taut = s2_ref[h, jsl, tsl], tau_ref[h, :, tsl]
                for ii in range(eb_rows):
                    i = eb * eb_rows + ii
                    c = s2t + s1_ref[h, pl.ds(i, 1), :][:, tsl]
                    accs[ii] = accs[ii] + jnp.where(c >= taut, jnp.exp2(c), 0.0)
            for ii in range(eb_rows):
                rsl = slice(ii * PEER_NKEYS + jh * JH, ii * PEER_NKEYS + (jh + 1) * JH)
                a = a_scr[rsl, tsl]
                g = 0.5 * a * (1.0 + lax.erf(a * (2.0 ** -0.5)))
                gw_scr[rsl, tsl] = (accs[ii] * g).astype(BF16)
    o_ref[...] += jnp.dot(vT_ref[...], gw_scr[...], preferred_element_type=F32)


def _peer(h2, w_q, sub_keys, u_bf, vT_bf):
    B, S, D = h2.shape
    N = B * S
    TT = 256 if N % 256 == 0 else LANES
    EB = 1024
    hT = h2.reshape(N, D).T
    keys = sub_keys.reshape(PEER_HEADS * 2, PEER_NKEYS, PEER_QDIM // 2) * LOG2_E
    stat = jax.ShapeDtypeStruct((PEER_HEADS, PEER_NKEYS, N), F32)
    stat_spec = pl.BlockSpec((PEER_HEADS, PEER_NKEYS, TT), lambda t: (0, 0, t))
    s1, s2, tau = pl.pallas_call(
        _peer_route_kernel,
        grid=(N // TT,),
        in_specs=[
            pl.BlockSpec((D, TT), lambda t: (0, t)),
            pl.BlockSpec((PEER_HEADS * PEER_QDIM, D), lambda t: (0, 0)),
            pl.BlockSpec((PEER_HEADS * 2, PEER_NKEYS, PEER_QDIM // 2), lambda t: (0, 0, 0)),
        ],
        out_specs=[stat_spec, stat_spec, pl.BlockSpec((PEER_HEADS, 1, TT), lambda t: (0, 0, t))],
        out_shape=[stat, stat, jax.ShapeDtypeStruct((PEER_HEADS, 1, N), F32)],
        compiler_params=pltpu.CompilerParams(
            dimension_semantics=("arbitrary",), vmem_limit_bytes=48 * 1024 * 1024),
        name="peer_route",
    )(hT, w_q.T, keys)

    TD = 1024 if N % 1024 == 0 else TT
    stat_spec2 = pl.BlockSpec((PEER_HEADS, PEER_NKEYS, TD), lambda t, e: (0, 0, t))
    oT = pl.pallas_call(
        functools.partial(_peer_dense_kernel, eb_rows=EB // PEER_NKEYS),
        grid=(N // TD, PEER_N // EB),
        in_specs=[
            pl.BlockSpec((D, TD), lambda t, e: (0, t)),
            pl.BlockSpec((EB, D), lambda t, e: (e, 0)),
            pl.BlockSpec((D, EB), lambda t, e: (0, e)),
            stat_spec2, stat_spec2,
            pl.BlockSpec((PEER_HEADS, 1, TD), lambda t, e: (0, 0, t)),
        ],
        out_specs=pl.BlockSpec((D, TD), lambda t, e: (0, t)),
        out_shape=jax.ShapeDtypeStruct((D, N), F32),
        scratch_shapes=[pltpu.VMEM((EB, TD), F32), pltpu.VMEM((EB, TD), BF16)],
        compiler_params=pltpu.CompilerParams(
            dimension_semantics=("arbitrary", "arbitrary"), vmem_limit_bytes=56 * 1024 * 1024),
        name="peer_dense",
    )(hT.astype(BF16), u_bf, vT_bf, s1, s2, tau)
    return oT.T.reshape(B, S, D)


def _gla(q, k, v, a_lr, r, w_a2, b_a, g_o):
    B, S, _ = q.shape
    nc = S // CHUNK
    H, dk, dv = GLA_HEADS, GLA_DK, GLA_DV
    q = q.reshape(B, nc, CHUNK, H, dk) * (dk ** -0.5)
    k = k.reshape(B, nc, CHUNK, H, dk)
    v = v.reshape(B, nc, CHUNK, H, dv)
    log_a = jax.nn.log_sigmoid(a_lr @ w_a2 + b_a) / GLA_TAU
    log_a = log_a.reshape(B, nc, CHUNK, H, dk)
    cum = jnp.cumsum(log_a, axis=2)
    tot = cum[:, :, -1]
    k_dec = k * jnp.exp(tot[:, :, None] - cum)
    upd = jnp.einsum('bnchk,bnchv->nbhkv', k_dec, v)
    chunk_decay = jnp.exp(tot).transpose(1, 0, 2, 3)

    def step(state, inp):
        d, u = inp
        state = d[..., None] * state + u
        return state, state

    _, states = lax.scan(step, jnp.zeros((B, H, dk, dv), F32), (chunk_decay, upd))
    o = jnp.einsum('bnchk,nbhkv->bnchv', q, states).reshape(B, S, H, dv)
    return _rms(o, g_o).reshape(B, S, H * dv) * jax.nn.silu(r)


def _dot_f32(x, y, dims=((1,), (0,))):
    xh, xl = x if isinstance(x, tuple) else _split_bf16(x)
    yh, yl = y if isinstance(y, tuple) else _split_bf16(y)
    mm = lambda p, q: lax.dot_general(p, q, (dims, ((), ())), preferred_element_type=F32)
    return mm(xh, yh) + (mm(xl, yh) + mm(xh, yl))


def _cumsum_rows(ltri_bf, x):
    h1, r1 = _split_bf16(x)
    h2 = r1.astype(F32)
    r2 = x - h1.astype(F32) - h2
    mm = lambda q: jnp.dot(ltri_bf, q, preferred_element_type=F32)
    return mm(h1) + (mm(r1) + mm(r2.astype(BF16)))


def _rwkv_chunk_kernel(r_ref, lw_ref, k_ref, v_ref, a_ref, b_ref, y_ref, st_scr):
    H, C = r_ref.shape[1], r_ref.shape[2]

    @pl.when(pl.program_id(1) == 0)
    def _():
        st_scr[...] = jnp.zeros(st_scr.shape, F32)

    row = lax.broadcasted_iota(jnp.int32, (C, C), 0)
    col = lax.broadcasted_iota(jnp.int32, (C, C), 1)
    lower = col <= row
    strict = col < row
    ltri = lower.astype(BF16)
    eye = (row == col).astype(F32)
    heads = range(H)
    each = lambda f, *xs: [f(*t) for t in zip(*xs)]
    nt = ((1,), (1,))
    lw, r, k, v, a, b = ([ref[0, h] for h in heads] for ref in (lw_ref, r_ref, k_ref, v_ref, a_ref, b_ref))
    s0 = [st_scr[h] for h in heads]
    cum = each(lambda x: _cumsum_rows(ltri, x), lw)
    g_in = each(jnp.exp, cum)
    g_inv = each(lambda c: jnp.exp(-c), cum)
    ar = each(lambda a_, r_, c, l, g: _split_bf16(jnp.concatenate([a_ * jnp.exp(c - l), r_ * g], axis=0)),
              a, r, cum, lw, g_in)
    bk = each(lambda b_, k_, g: jnp.concatenate([b_ * g, k_ * g], axis=0), b, k, g_inv)
    v_s = each(_split_bf16, v)
    p = each(lambda x, y: _dot_f32(x, y, nt), ar, bk)
    ars = each(lambda x, s: _dot_f32(x, s, nt), ar, s0)
    ab = each(lambda q: jnp.where(strict, q[:C, :C], 0.0), p)
    ak = each(lambda q: jnp.where(strict, q[:C, C:], 0.0), p)
    rb = each(lambda q: jnp.where(lower, q[C:, :C], 0.0), p)
    rk = each(lambda q: jnp.where(lower, q[C:, C:], 0.0), p)
    rhs = each(lambda x, m, vs: x[:C] + _dot_f32(m, vs), ars, ak, v_s)
    inv, pw = each(lambda m: eye + m, ab), ab
    for _ in range(int(np.log2(C)) - 1):
        pw = each(lambda m: (lambda s: _dot_f32(s, s))(_split_bf16(m)), pw)
        inv = each(lambda i_, m: i_ + _dot_f32(i_, m), inv, pw)
    u = each(_dot_f32, inv, rhs)
    y = each(lambda x, m1, u_, m2, vs: x[C:] + _dot_f32(m1, u_) + _dot_f32(m2, vs), ars, rb, u, rk, v_s)
    for h in heads:
        y_ref[0, h] = y[h]
    g_end = each(lambda c: jnp.exp(c[C - 1:C] - c), cum)
    uv = each(lambda u_, v_: jnp.concatenate([u_, v_], axis=0), u, v)
    bkd = each(lambda b_, k_, g: jnp.concatenate([b_ * g, k_ * g], axis=0), b, k, g_end)
    s1 = each(lambda s, g, x, z: s * g[C - 1:C] + _dot_f32(x, z, ((0,), (0,))), s0, g_in, uv, bkd)
    for h in heads:
        st_scr[h] = s1[h]


def _rwkv7(p, mu, w0, w_w2, a0, w_a2, w_g2, k_k, k_a, r_k, ln_g, ln_b):
    B, S, _ = p.shape
    H, N = RWKV_HEADS, RWKV_HEAD
    prev = jnp.pad(p, ((0, 0), (1, 0), (0, 0)))[:, :-1]
    p = p + (prev - p) * mu
    r, k, v, w_lr, a_lr, g_lr = _split(p, RWKV_SIZES)
    log_w = -jax.nn.softplus(-(w0 + jnp.tanh(w_lr) @ w_w2)) - 0.5
    log_decay = -jnp.exp(log_w)
    a = jax.nn.sigmoid(a0 + a_lr @ w_a2)
    g = jax.nn.sigmoid(g_lr) @ w_g2
    kk = (k * k_k).reshape(B, S, H, N)
    kk = kk * lax.rsqrt(jnp.sum(kk * kk, axis=-1, keepdims=True) + 1e-12)
    k = k * (1.0 + (a - 1.0) * k_a)
    r_h, k_h, v_h, a_h, lw_h = [t.reshape(B, S, H, N) for t in (r, k, v, a, log_decay)]
    heads_first = lambda t: jnp.swapaxes(t, 1, 2)
    C = RWKV_CHUNK
    blk = pl.BlockSpec((1, H, C, N), lambda b, c: (b, 0, c, 0))
    y = pl.pallas_call(
        _rwkv_chunk_kernel,
        grid=(B, S // C),
        in_specs=[blk] * 6,
        out_specs=blk,
        out_shape=jax.ShapeDtypeStruct((B, H, S, N), F32),
        scratch_shapes=[pltpu.VMEM((H, N, N), F32)],
        compiler_params=pltpu.CompilerParams(dimension_semantics=("arbitrary", "arbitrary")),
        name="rwkv7_chunk",
    )(*[heads_first(t) for t in (r_h, lw_h, k_h, v_h, -kk, kk * a_h)])
    y = jnp.swapaxes(y, 1, 2)
    mean = jnp.mean(y, axis=-1, keepdims=True)
    var = jnp.mean(jnp.square(y - mean), axis=-1, keepdims=True)
    yn = ((y - mean) * lax.rsqrt(var + RWKV_LN_EPS)).reshape(B, S, H * N) * ln_g + ln_b
    bonus = (jnp.sum(r_h * k_h * r_k, axis=-1, keepdims=True) * v_h).reshape(B, S, H * N)
    return (yn + bonus) * g


def kernel(x, c, w_ada, b_ada, g_norm_mix, g_norm_ffn, w_in, gla_w_a2, gla_b_a, gla_g_o, dsa_g_qc, dsa_w_uq, dsa_w_iq, dsa_g_kv, dsa_w_uv, rwkv_mu, rwkv_w0, rwkv_w_w2, rwkv_a0, rwkv_w_a2, rwkv_w_g2, rwkv_k_k, rwkv_k_a, rwkv_r_k, rwkv_ln_g, rwkv_ln_b, w_branch, w_out, peer_w_q, peer_keys, peer_u, peer_v, g_final):
    B, S, D = x.shape
    depth = w_in.shape[0]
    in_sizes = _in_sizes(D)
    cond = jax.nn.silu(c)
    u_bf = peer_u.astype(BF16)
    vT_bf = jnp.swapaxes(peer_v, 1, 2).astype(BF16)
    for l in range(depth):
        mod = cond @ w_ada[l] + b_ada[l]
        sh_m, sc_m, gt_m, sh_f, sc_f, gt_f = [m[:, None, :] for m in jnp.split(mod, 6, axis=-1)]
        h = _rms(x, g_norm_mix[l]) * (1.0 + sc_m) + sh_m
        gq, gk, gv, galr, gr, qc, kvc, ik, iw, prw, gates = _split(h @ w_in[l], in_sizes)
        y_a = _gla(gq, gk, gv, galr, gr, gla_w_a2[l], gla_b_a[l], gla_g_o[l])
        y_b = _dsa(qc, kvc, ik, iw, dsa_g_qc[l], dsa_w_uq[l], dsa_w_iq[l], dsa_g_kv[l], dsa_w_uv[l])
        y_c = _rwkv7(prw, rwkv_mu[l], rwkv_w0[l], rwkv_w_w2[l], rwkv_a0[l], rwkv_w_a2[l], rwkv_w_g2[l],
                     rwkv_k_k[l], rwkv_k_a[l], rwkv_r_k[l], rwkv_ln_g[l], rwkv_ln_b[l])
        g_a, g_b, g_c = jnp.split(jax.nn.sigmoid(gates), N_BRANCH, axis=-1)
        merged = (g_a * (y_a @ w_branch[l, 0]) + g_b * (y_b @ w_branch[l, 1]) + g_c * (y_c @ w_branch[l, 2]))
        x = x + gt_m * (merged @ w_out[l])
        h2 = _rms(x, g_norm_ffn[l]) * (1.0 + sc_f) + sh_f
        x = x + gt_f * _peer(h2, peer_w_q[l], peer_keys[l], u_bf[l], vT_bf[l])
    return _rms(x, g_final)
```

```python
import functools

import jax
import jax.numpy as jnp
import numpy as np
from jax import lax
from jax.experimental import pallas as pl
from jax.experimental.pallas import tpu as pltpu

CHUNK = 64
Q_BLOCK = 256
NORM_EPS = 1e-6

GLA_HEADS, GLA_DK, GLA_DV, GLA_GATE_RANK, GLA_TAU = 4, 64, 128, 16, 16.0
DSA_HEADS, DSA_Q_RANK, DSA_LAT, DSA_DV = 8, 256, 128, 64
IDX_HEADS, IDX_DIM, DSA_TOPK_MAX = 4, 64, 256
RWKV_HEADS, RWKV_HEAD = 8, 64
RWKV_W_RANK, RWKV_A_RANK, RWKV_G_RANK = 64, 64, 128
RWKV_LN_EPS = 64e-5
RWKV_CHUNK = 64
RWKV_WR = RWKV_HEADS * RWKV_HEAD
RWKV_SIZES = (RWKV_WR, RWKV_WR, RWKV_WR, RWKV_W_RANK, RWKV_A_RANK, RWKV_G_RANK)
RWKV_WIDTH = sum(RWKV_SIZES)
N_BRANCH = 3
PEER_HEADS, PEER_NKEYS, PEER_QDIM, PEER_TOPK = 8, 128, 128, 16
PEER_N = PEER_NKEYS * PEER_NKEYS

LANES = 128
DSA_KEY_BLOCK = 512
INT_MIN = int(np.iinfo(np.int32).min)
LOG2_E = float(np.log2(np.e))
F32 = jnp.float32
BF16 = jnp.bfloat16
HIGHEST = lax.Precision.HIGHEST


def _in_sizes(d_model):
    return (GLA_HEADS * GLA_DK, GLA_HEADS * GLA_DK, GLA_HEADS * GLA_DV, GLA_GATE_RANK, GLA_HEADS * GLA_DV,
            DSA_Q_RANK, DSA_LAT, IDX_DIM, IDX_HEADS, RWKV_WIDTH, N_BRANCH * d_model)


def _rms(x, g, eps=NORM_EPS):
    xf = x.astype(F32)
    y = xf * lax.rsqrt(jnp.mean(xf * xf, axis=-1, keepdims=True) + eps)
    return (y * g).astype(x.dtype)


def _split(p, sizes):
    cuts = [int(s) for s in np.cumsum(sizes)[:-1]]
    return jnp.split(p, cuts, axis=-1)


def _split_bf16(x):
    hi = x.astype(BF16)
    return hi, (x - hi.astype(F32)).astype(BF16)


def _sortable_key(x):
    b = lax.bitcast_convert_type(x, jnp.int32)
    return b ^ ((b >> 31) & jnp.int32(0x7FFFFFFF))


def _dsa_kernel(q3_ref, wi_ref, q_ref, k3_ref, kvT_ref, kv_ref, wuv_ref, o_ref,
                key_scr, m_scr, l_scr, acc_scr, *, topk):
    QB, KB = Q_BLOCK, DSA_KEY_BLOCK
    bi = pl.program_id(1)
    nkb = (bi * QB + QB + KB - 1) // KB
    row = lax.broadcasted_iota(jnp.int32, (QB, KB), 0)
    lane = lax.broadcasted_iota(jnp.int32, (QB, KB), 1)
    q_end = ((bi * QB + row) // CHUNK + 1) * CHUNK
    wb = [jnp.broadcast_to(wi_ref[0, :, h:h + 1], (QB, KB)) for h in range(IDX_HEADS)]
    q3 = q3_ref[0].reshape(IDX_HEADS * QB, 3 * IDX_DIM)

    def key_off(kb):
        return pl.multiple_of(kb * KB, KB)

    def fold(x):
        return sum(x[:, g * LANES:(g + 1) * LANES] for g in range(1, KB // LANES)) + x[:, :LANES]

    def score_body(kb, c):
        s = jnp.dot(q3, k3_ref[0, :, pl.ds(key_off(kb), KB)], preferred_element_type=F32)
        sc = jnp.zeros((QB, KB), F32)
        for h in range(IDX_HEADS):
            sc = sc + jnp.maximum(s[h * QB:(h + 1) * QB], 0.0) * wb[h]
        key_scr[kb] = jnp.where(key_off(kb) + lane < q_end, _sortable_key(sc), INT_MIN)
        return c

    lax.fori_loop(0, nkb, score_body, 0)

    def count(pred, thr):
        def body(kb, acc):
            return acc + fold(pred(key_scr[kb], thr).astype(jnp.int32))
        acc = lax.fori_loop(0, nkb, body, jnp.zeros((QB, LANES), jnp.int32))
        return jnp.sum(acc, axis=1, keepdims=True)

    n_adm = q_end[:, 0:1]

    def bit_body(i, st):
        thr, cnt = st
        cand = thr + jnp.left_shift(jnp.int32(1), 31 - i)
        c = count(lambda k, t: k >= t, cand)
        take = c >= topk
        return jnp.where(take, cand, thr), jnp.where(take, c, cnt)

    thr, cnt = lax.fori_loop(0, 32, bit_body, (jnp.full((QB, 1), INT_MIN, jnp.int32), n_adm))

    def tie_path():
        n_gt = count(lambda k, t: k > t, thr)
        need = (topk - n_gt).astype(F32)
        r2 = lax.broadcasted_iota(jnp.int32, (KB, KB), 0)
        c2 = lax.broadcasted_iota(jnp.int32, (KB, KB), 1)
        tri = (r2 <= c2).astype(BF16)
        ones = jnp.ones((KB, KB), BF16)

        def mask_body(kb, carry):
            key = key_scr[kb]
            eq = key == thr
            eqb = eq.astype(BF16)
            pre = jnp.dot(eqb, tri, preferred_element_type=F32)
            sel = ((key > thr) | (eq & (carry + pre <= need))) & (key > INT_MIN)
            key_scr[kb] = jnp.where(sel, jnp.int32(1), jnp.int32(INT_MIN))
            return carry + jnp.dot(eqb, ones, preferred_element_type=F32)

        lax.fori_loop(0, nkb, mask_body, jnp.zeros((QB, KB), F32))
        return jnp.zeros((QB, 1), jnp.int32)

    thr_sel = lax.cond(jnp.max(jnp.where(cnt > topk, 1, 0)) > 0, tie_path,
                       lambda: jnp.maximum(thr, INT_MIN + 1))

    scale = DSA_LAT ** -0.5

    def logits(kb):
        return jnp.dot(q_ref[0, 0], kvT_ref[0, :, pl.ds(key_off(kb), KB)], preferred_element_type=F32)

    m_scr[...] = jnp.full(m_scr.shape, -jnp.inf, F32)

    def max_body(kb, c):
        sel = key_scr[kb] >= thr_sel
        lg = logits(kb)
        for h in range(DSA_HEADS):
            x = jnp.where(sel, lg[h * QB:(h + 1) * QB], -jnp.inf)
            xm = x[:, :LANES]
            for g in range(1, KB // LANES):
                xm = jnp.maximum(xm, x[:, g * LANES:(g + 1) * LANES])
            m_scr[h] = jnp.maximum(m_scr[h], xm)
        return c

    lax.fori_loop(0, nkb, max_body, 0)
    for h in range(DSA_HEADS):
        m_scr[h] = jnp.broadcast_to(jnp.max(m_scr[h], axis=1, keepdims=True), (QB, LANES))
    l_scr[...] = jnp.zeros(l_scr.shape, F32)
    acc_scr[...] = jnp.zeros(acc_scr.shape, F32)

    def acc_body(kb, c):
        sel = key_scr[kb] >= thr_sel
        kvb = kv_ref[0, pl.ds(key_off(kb), KB), :]
        lg = logits(kb)
        ps = []
        for h in range(DSA_HEADS):
            mb = jnp.concatenate([m_scr[h]] * (KB // LANES), axis=1)
            p = jnp.where(sel, jnp.exp2((lg[h * QB:(h + 1) * QB] - mb) * (scale * LOG2_E)), 0.0)
            l_scr[h] += fold(p)
            ps.append(p.astype(BF16))
        acc_scr[...] += jnp.dot(jnp.concatenate(ps, axis=0), kvb, preferred_element_type=F32)
        return c

    lax.fori_loop(0, nkb, acc_body, 0)
    for h in range(DSA_HEADS):
        o = acc_scr[h * QB:(h + 1) * QB] / jnp.sum(l_scr[h], axis=1, keepdims=True)
        o_ref[0, :, h * DSA_DV:(h + 1) * DSA_DV] = jnp.dot(
            o.astype(BF16), wuv_ref[h], preferred_element_type=F32)


def _dsa(qc, kvc, idx_k, idx_w, g_qc, w_uq, w_iq, g_kv, w_uv):
    B, S, _ = kvc.shape
    topk = min(DSA_TOPK_MAX, S // 4)
    nb = S // Q_BLOCK
    qcn = _rms(qc, g_qc)
    q = jnp.dot(qcn, w_uq).astype(BF16)
    q_rows = q.reshape(B, nb, Q_BLOCK, DSA_HEADS, DSA_LAT).transpose(0, 1, 3, 2, 4).reshape(
        B, nb, DSA_HEADS * Q_BLOCK, DSA_LAT)
    qi = jnp.dot(qcn, w_iq, precision=HIGHEST)
    qi_t = qi.reshape(B, S, IDX_HEADS, IDX_DIM).transpose(0, 2, 1, 3)
    q_hi, q_lo = _split_bf16(qi_t)
    q3 = jnp.concatenate([q_hi, q_lo, q_hi], axis=-1)
    k_hi, k_lo = _split_bf16(idx_k.transpose(0, 2, 1))
    k3 = jnp.concatenate([k_hi, k_hi, k_lo], axis=1)
    kv = _rms(kvc, g_kv).astype(BF16)
    kvT = kv.transpose(0, 2, 1)
    return pl.pallas_call(
        functools.partial(_dsa_kernel, topk=topk),
        grid=(B, nb),
        in_specs=[
            pl.BlockSpec((1, IDX_HEADS, Q_BLOCK, 3 * IDX_DIM), lambda b, i: (b, 0, i, 0)),
            pl.BlockSpec((1, Q_BLOCK, IDX_HEADS), lambda b, i: (b, i, 0)),
            pl.BlockSpec((1, 1, DSA_HEADS * Q_BLOCK, DSA_LAT), lambda b, i: (b, i, 0, 0)),
            pl.BlockSpec((1, 3 * IDX_DIM, S), lambda b, i: (b, 0, 0)),
            pl.BlockSpec((1, DSA_LAT, S), lambda b, i: (b, 0, 0)),
            pl.BlockSpec((1, S, DSA_LAT), lambda b, i: (b, 0, 0)),
            pl.BlockSpec((DSA_HEADS, DSA_LAT, DSA_DV), lambda b, i: (0, 0, 0)),
        ],
        out_specs=pl.BlockSpec((1, Q_BLOCK, DSA_HEADS * DSA_DV), lambda b, i: (b, i, 0)),
        out_shape=jax.ShapeDtypeStruct((B, S, DSA_HEADS * DSA_DV), F32),
        scratch_shapes=[
            pltpu.VMEM((S // DSA_KEY_BLOCK, Q_BLOCK, DSA_KEY_BLOCK), jnp.int32),
            pltpu.VMEM((DSA_HEADS, Q_BLOCK, LANES), F32),
            pltpu.VMEM((DSA_HEADS, Q_BLOCK, LANES), F32),
            pltpu.VMEM((DSA_HEADS * Q_BLOCK, DSA_LAT), F32),
        ],
        compiler_params=pltpu.CompilerParams(
            dimension_semantics=("arbitrary", "arbitrary"),
            vmem_limit_bytes=48 * 1024 * 1024),
        name="dsa_fused",
    )(q3, idx_w, q_rows, k3, kvT, kv, w_uv.astype(BF16))


SUBLANES = 8


def _merge_exchange_pairs(n):
    t = int(np.ceil(np.log2(n)))
    p, pairs = 1 << (t - 1), []
    while p > 0:
        q, r, d = 1 << (t - 1), 0, p
        while d > 0:
            pairs += [(i, i + d) for i in range(n - d) if (i & p) == r]
            d, q, r = q - p, q >> 1, p
        p >>= 1
    return pairs


def _exchange(rows, pairs):
    rows = list(rows)
    for i, j in pairs:
        rows[i], rows[j] = jnp.maximum(rows[i], rows[j]), jnp.minimum(rows[i], rows[j])
    return rows


def _merge_sublane_lists(rows):
    k = len(rows)
    clean = [(i, i + d) for d in (k >> s for s in range(1, k.bit_length())) for i in range(k) if (i & d) == 0]
    for sh in (4, 2, 1):
        far = [pltpu.roll(r, sh, axis=0) for r in rows]
        rows = _exchange([jnp.maximum(rows[i], far[k - 1 - i]) for i in range(k)], clean)
    return rows


def _top_sorted(s, k):
    rows = [s[SUBLANES * v:SUBLANES * (v + 1)] for v in range(s.shape[0] // SUBLANES)]
    assert len(rows) == k
    return _merge_sublane_lists(_exchange(rows, _merge_exchange_pairs(k)))


def _peer_route_kernel(hT_ref, wqT_ref, keys_ref, s1_ref, s2_ref, tau_ref):
    half = PEER_QDIM // 2
    K = PEER_TOPK
    TT = hT_ref.shape[1]
    sub = lax.broadcasted_iota(jnp.int32, (SUBLANES, TT), 0)

    def spread(rows):
        out = rows[SUBLANES - 1]
        for j in range(SUBLANES - 2, -1, -1):
            out = jnp.where(sub == j, rows[j], out)
        return out

    qT = jnp.dot(wqT_ref[...], hT_ref[...], precision=HIGHEST, preferred_element_type=F32)
    for h in range(PEER_HEADS):
        s = []
        for p in range(2):
            r0 = (h * 2 + p) * half
            s.append(jnp.dot(keys_ref[h * 2 + p], qT[r0:r0 + half], precision=HIGHEST,
                             preferred_element_type=F32))
        a = _top_sorted(s[0], K)
        b = _top_sorted(s[1], K)
        b_lo, b_hi = spread(b[:SUBLANES]), spread(b[SUBLANES:])
        c = [a[i] + b_lo for i in range(K)]
        x = a[0] + b_hi
        for i in range(K):
            c[i], x = jnp.maximum(c[i], x), jnp.minimum(c[i], x)
        top = _merge_sublane_lists(c)
        tau = top[K - 1]
        z = sum(jnp.exp2(t - top[0]) for t in top)
        shift = top[0] + jnp.log2(z)
        s1_ref[h] = s[0] - shift[0:1]
        s2_ref[h] = s[1]
        low = jnp.where(a[0] + b_hi >= tau, (a[0] - shift) + b_hi, jnp.inf)
        for i in range(K):
            low = jnp.minimum(low, jnp.where(a[i] + b_lo >= tau, (a[i] - shift) + b_lo, jnp.inf))
        tau_ref[h] = jnp.min(low, axis=0, keepdims=True)


def _peer_dense_kernel(hT_ref, u_ref, vT_ref, s1_ref, s2_ref, tau_ref, o_ref, a_scr, gw_scr, *, eb_rows):
    eb = pl.program_id(1)
    TT = hT_ref.shape[1]

    @pl.when(eb == 0)
    def _():
        o_ref[...] = jnp.zeros(o_ref.shape, F32)

    a_scr[...] = jnp.dot(u_ref[...], hT_ref[...], preferred_element_type=F32)
    JH = PEER_NKEYS // 4
    for ts in range(TT // LANES):
        tsl = slice(ts * LANES, (ts + 1) * LANES)
        for jh in range(PEER_NKEYS // JH):
            jsl = slice(jh * JH, (jh + 1) * JH)
            accs = [jnp.zeros((JH, LANES), F32) for _ in range(eb_rows)]
            for h in range(PEER_HEADS):
                s2t, taut = s2_ref[h, jsl, tsl], tau_ref[h, :, tsl]
                for ii in range(eb_rows):
                    i = eb * eb_rows + ii
                    c = s2t + s1_ref[h, pl.ds(i, 1), :][:, tsl]
                    accs[ii] = accs[ii] + jnp.where(c >= taut, jnp.exp2(c), 0.0)
            for ii in range(eb_rows):
                rsl = slice(ii * PEER_NKEYS + jh * JH, ii * PEER_NKEYS + (jh + 1) * JH)
                a = a_scr[rsl, tsl]
                g = 0.5 * a * (1.0 + lax.erf(a * (2.0 ** -0.5)))
                gw_scr[rsl, tsl] = (accs[ii] * g).astype(BF16)
    o_ref[...] += jnp.dot(vT_ref[...], gw_scr[...], preferred_element_type=F32)


def _peer(h2, w_q, sub_keys, u_bf, vT_bf):
    B, S, D = h2.shape
    N = B * S
    TT = 256 if N % 256 == 0 else LANES
    EB = 1024
    hT = h2.reshape(N, D).T
    keys = sub_keys.reshape(PEER_HEADS * 2, PEER_NKEYS, PEER_QDIM // 2) * LOG2_E
    stat = jax.ShapeDtypeStruct((PEER_HEADS, PEER_NKEYS, N), F32)
    stat_spec = pl.BlockSpec((PEER_HEADS, PEER_NKEYS, TT), lambda t: (0, 0, t))
    s1, s2, tau = pl.pallas_call(
        _peer_route_kernel,
        grid=(N // TT,),
        in_specs=[
            pl.BlockSpec((D, TT), lambda t: (0, t)),
            pl.BlockSpec((PEER_HEADS * PEER_QDIM, D), lambda t: (0, 0)),
            pl.BlockSpec((PEER_HEADS * 2, PEER_NKEYS, PEER_QDIM // 2), lambda t: (0, 0, 0)),
        ],
        out_specs=[stat_spec, stat_spec, pl.BlockSpec((PEER_HEADS, 1, TT), lambda t: (0, 0, t))],
        out_shape=[stat, stat, jax.ShapeDtypeStruct((PEER_HEADS, 1, N), F32)],
        compiler_params=pltpu.CompilerParams(
            dimension_semantics=("arbitrary",), vmem_limit_bytes=48 * 1024 * 1024),
        name="peer_route",
    )(hT, w_q.T, keys)

    TD = 1024 if N % 1024 == 0 else TT
    stat_spec2 = pl.BlockSpec((PEER_HEADS, PEER_NKEYS, TD), lambda t, e: (0, 0, t))
    oT = pl.pallas_call(
        functools.partial(_peer_dense_kernel, eb_rows=EB // PEER_NKEYS),
        grid=(N // TD, PEER_N // EB),
        in_specs=[
            pl.BlockSpec((D, TD), lambda t, e: (0, t)),
            pl.BlockSpec((EB, D), lambda t, e: (e, 0)),
            pl.BlockSpec((D, EB), lambda t, e: (0, e)),
            stat_spec2, stat_spec2,
            pl.BlockSpec((PEER_HEADS, 1, TD), lambda t, e: (0, 0, t)),
        ],
        out_specs=pl.BlockSpec((D, TD), lambda t, e: (0, t)),
        out_shape=jax.ShapeDtypeStruct((D, N), F32),
        scratch_shapes=[pltpu.VMEM((EB, TD), F32), pltpu.VMEM((EB, TD), BF16)],
        compiler_params=pltpu.CompilerParams(
            dimension_semantics=("arbitrary", "arbitrary"), vmem_limit_bytes=56 * 1024 * 1024),
        name="peer_dense",
    )(hT.astype(BF16), u_bf, vT_bf, s1, s2, tau)
    return oT.T.reshape(B, S, D)


def _gla(q, k, v, a_lr, r, w_a2, b_a, g_o):
    B, S, _ = q.shape
    nc = S // CHUNK
    H, dk, dv = GLA_HEADS, GLA_DK, GLA_DV
    q = q.reshape(B, nc, CHUNK, H, dk) * (dk ** -0.5)
    k = k.reshape(B, nc, CHUNK, H, dk)
    v = v.reshape(B, nc, CHUNK, H, dv)
    log_a = jax.nn.log_sigmoid(a_lr @ w_a2 + b_a) / GLA_TAU
    log_a = log_a.reshape(B, nc, CHUNK, H, dk)
    cum = jnp.cumsum(log_a, axis=2)
    tot = cum[:, :, -1]
    k_dec = k * jnp.exp(tot[:, :, None] - cum)
    upd = jnp.einsum('bnchk,bnchv->nbhkv', k_dec, v)
    chunk_decay = jnp.exp(tot).transpose(1, 0, 2, 3)

    def step(state, inp):
        d, u = inp
        state = d[..., None] * state + u
        return state, state

    _, states = lax.scan(step, jnp.zeros((B, H, dk, dv), F32), (chunk_decay, upd))
    o = jnp.einsum('bnchk,nbhkv->bnchv', q, states).reshape(B, S, H, dv)
    return _rms(o, g_o).reshape(B, S, H * dv) * jax.nn.silu(r)


def _dot_f32(x, y, dims=((1,), (0,))):
    xh, xl = x if isinstance(x, tuple) else _split_bf16(x)
    yh, yl = y if isinstance(y, tuple) else _split_bf16(y)
    mm = lambda p, q: lax.dot_general(p, q, (dims, ((), ())), preferred_element_type=F32)
    return mm(xh, yh) + (mm(xl, yh) + mm(xh, yl))


def _cumsum_rows(ltri_bf, x):
    h1, r1 = _split_bf16(x)
    h2 = r1.astype(F32)
    r2 = x - h1.astype(F32) - h2
    mm = lambda q: jnp.dot(ltri_bf, q, preferred_element_type=F32)
    return mm(h1) + (mm(r1) + mm(r2.astype(BF16)))


def _rwkv_chunk_kernel(r_ref, lw_ref, k_ref, v_ref, a_ref, b_ref, y_ref, st_scr):
    H, C = r_ref.shape[1], r_ref.shape[2]

    @pl.when(pl.program_id(1) == 0)
    def _():
        st_scr[...] = jnp.zeros(st_scr.shape, F32)

    row = lax.broadcasted_iota(jnp.int32, (C, C), 0)
    col = lax.broadcasted_iota(jnp.int32, (C, C), 1)
    lower = col <= row
    strict = col < row
    ltri = lower.astype(BF16)
    eye = (row == col).astype(F32)
    heads = range(H)
    each = lambda f, *xs: [f(*t) for t in zip(*xs)]
    nt = ((1,), (1,))
    lw, r, k, v, a, b = ([ref[0, h] for h in heads] for ref in (lw_ref, r_ref, k_ref, v_ref, a_ref, b_ref))
    s0 = [st_scr[h] for h in heads]
    cum = each(lambda x: _cumsum_rows(ltri, x), lw)
    g_in = each(jnp.exp, cum)
    g_inv = each(lambda c: jnp.exp(-c), cum)
    ar = each(lambda a_, r_, c, l, g: _split_bf16(jnp.concatenate([a_ * jnp.exp(c - l), r_ * g], axis=0)),
              a, r, cum, lw, g_in)
    bk = each(lambda b_, k_, g: jnp.concatenate([b_ * g, k_ * g], axis=0), b, k, g_inv)
    v_s = each(_split_bf16, v)
    p = each(lambda x, y: _dot_f32(x, y, nt), ar, bk)
    ars = each(lambda x, s: _dot_f32(x, s, nt), ar, s0)
    ab = each(lambda q: jnp.where(strict, q[:C, :C], 0.0), p)
    ak = each(lambda q: jnp.where(strict, q[:C, C:], 0.0), p)
    rb = each(lambda q: jnp.where(lower, q[C:, :C], 0.0), p)
    rk = each(lambda q: jnp.where(lower, q[C:, C:], 0.0), p)
    rhs = each(lambda x, m, vs: x[:C] + _dot_f32(m, vs), ars, ak, v_s)
    inv, pw = each(lambda m: eye + m, ab), ab
    for _ in range(int(np.log2(C)) - 1):
        pw = each(lambda m: (lambda s: _dot_f32(s, s))(_split_bf16(m)), pw)
        inv = each(lambda i_, m: i_ + _dot_f32(i_, m), inv, pw)
    u = each(_dot_f32, inv, rhs)
    y = each(lambda x, m1, u_, m2, vs: x[C:] + _dot_f32(m1, u_) + _dot_f32(m2, vs), ars, rb, u, rk, v_s)
    for h in heads:
        y_ref[0, h] = y[h]
    g_end = each(lambda c: jnp.exp(c[C - 1:C] - c), cum)
    uv = each(lambda u_, v_: jnp.concatenate([u_, v_], axis=0), u, v)
    bkd = each(lambda b_, k_, g: jnp.concatenate([b_ * g, k_ * g], axis=0), b, k, g_end)
    s1 = each(lambda s, g, x, z: s * g[C - 1:C] + _dot_f32(x, z, ((0,), (0,))), s0, g_in, uv, bkd)
    for h in heads:
        st_scr[h] = s1[h]


def _rwkv7(p, mu, w0, w_w2, a0, w_a2, w_g2, k_k, k_a, r_k, ln_g, ln_b):
    B, S, _ = p.shape
    H, N = RWKV_HEADS, RWKV_HEAD
    prev = jnp.pad(p, ((0, 0), (1, 0), (0, 0)))[:, :-1]
    p = p + (prev - p) * mu
    r, k, v, w_lr, a_lr, g_lr = _split(p, RWKV_SIZES)
    log_w = -jax.nn.softplus(-(w0 + jnp.tanh(w_lr) @ w_w2)) - 0.5
    log_decay = -jnp.exp(log_w)
    a = jax.nn.sigmoid(a0 + a_lr @ w_a2)
    g = jax.nn.sigmoid(g_lr) @ w_g2
    kk = (k * k_k).reshape(B, S, H, N)
    kk = kk * lax.rsqrt(jnp.sum(kk * kk, axis=-1, keepdims=True) + 1e-12)
    k = k * (1.0 + (a - 1.0) * k_a)
    r_h, k_h, v_h, a_h, lw_h = [t.reshape(B, S, H, N) for t in (r, k, v, a, log_decay)]
    heads_first = lambda t: jnp.swapaxes(t, 1, 2)
    C = RWKV_CHUNK
    blk = pl.BlockSpec((1, H, C, N), lambda b, c: (b, 0, c, 0))
    y = pl.pallas_call(
        _rwkv_chunk_kernel,
        grid=(B, S // C),
        in_specs=[blk] * 6,
        out_specs=blk,
        out_shape=jax.ShapeDtypeStruct((B, H, S, N), F32),
        scratch_shapes=[pltpu.VMEM((H, N, N), F32)],
        compiler_params=pltpu.CompilerParams(dimension_semantics=("arbitrary", "arbitrary")),
        name="rwkv7_chunk",
    )(*[heads_first(t) for t in (r_h, lw_h, k_h, v_h, -kk, kk * a_h)])
    y = jnp.swapaxes(y, 1, 2)
    mean = jnp.mean(y, axis=-1, keepdims=True)
    var = jnp.mean(jnp.square(y - mean), axis=-1, keepdims=True)
    yn = ((y - mean) * lax.rsqrt(var + RWKV_LN_EPS)).reshape(B, S, H * N) * ln_g + ln_b
    bonus = (jnp.sum(r_h * k_h * r_k, axis=-1, keepdims=True) * v_h).reshape(B, S, H * N)
    return (yn + bonus) * g


def kernel(x, c, w_ada, b_ada, g_norm_mix, g_norm_ffn, w_in, gla_w_a2, gla_b_a, gla_g_o, dsa_g_qc, dsa_w_uq, dsa_w_iq, dsa_g_kv, dsa_w_uv, rwkv_mu, rwkv_w0, rwkv_w_w2, rwkv_a0, rwkv_w_a2, rwkv_w_g2, rwkv_k_k, rwkv_k_a, rwkv_r_k, rwkv_ln_g, rwkv_ln_b, w_branch, w_out, peer_w_q, peer_keys, peer_u, peer_v, g_final):
    B, S, D = x.shape
    depth = w_in.shape[0]
    in_sizes = _in_sizes(D)
    cond = jax.nn.silu(c)
    u_bf = peer_u.astype(BF16)
    vT_bf = jnp.swapaxes(peer_v, 1, 2).astype(BF16)
    for l in range(depth):
        mod = cond @ w_ada[l] + b_ada[l]
        sh_m, sc_m, gt_m, sh_f, sc_f, gt_f = [m[:, None, :] for m in jnp.split(mod, 6, axis=-1)]
        h = _rms(x, g_norm_mix[l]) * (1.0 + sc_m) + sh_m
        gq, gk, gv, galr, gr, qc, kvc, ik, iw, prw, gates = _split(h @ w_in[l], in_sizes)
        y_a = _gla(gq, gk, gv, galr, gr, gla_w_a2[l], gla_b_a[l], gla_g_o[l])
        y_b = _dsa(qc, kvc, ik, iw, dsa_g_qc[l], dsa_w_uq[l], dsa_w_iq[l], dsa_g_kv[l], dsa_w_uv[l])
        y_c = _rwkv7(prw, rwkv_mu[l], rwkv_w0[l], rwkv_w_w2[l], rwkv_a0[l], rwkv_w_a2[l], rwkv_w_g2[l],
                     rwkv_k_k[l], rwkv_k_a[l], rwkv_r_k[l], rwkv_ln_g[l], rwkv_ln_b[l])
        g_a, g_b, g_c = jnp.split(jax.nn.sigmoid(gates), N_BRANCH, axis=-1)
        merged = (g_a * (y_a @ w_branch[l, 0]) + g_b * (y_b @ w_branch[l, 1]) + g_c * (y_c @ w_branch[l, 2]))
        x = x + gt_m * (merged @ w_out[l])
        h2 = _rms(x, g_norm_ffn[l]) * (1.0 + sc_f) + sh_f
        x = x + gt_f * _peer(h2, peer_w_q[l], peer_keys[l], u_bf[l], vT_bf[l])
    return _rms(x, g_final)
```
